```python
import math
import jax, jax.numpy as jnp
from jax import lax
import numpy as np

D_MODEL = 1024
BATCH = 8
SEQ = 4096
DEPTH = 2

N_A_LAYERS = (DEPTH + 1) // 2
N_B_LAYERS = DEPTH - N_A_LAYERS
HEAD_DIM = 64
N_HEADS = D_MODEL // (2 * HEAD_DIM)
DIL_GROUPS = ((128, 1), (512, 4), (2048, 16))
N_GROUPS = len(DIL_GROUPS)
ATTN_BLOCK = 128
DIFF_V_DIM = 2 * HEAD_DIM
D_FF = ((8 * D_MODEL // 3 + 127) // 128) * 128
CONV_WIDTH = 3
NUM_BUCKETS = 32
MAX_DISTANCE = 2048
RMS_EPS = 1e-6
SUBLN_EPS = 1e-5

kernel_name = 'hybrid_dilated_diffattn_yoco'


def rms_norm(x, g, eps=RMS_EPS):
    xf = x.astype(jnp.float32)
    y = xf * lax.rsqrt(jnp.mean(xf * xf, axis=-1, keepdims=True) + eps)
    return (y * g.astype(jnp.float32)).astype(x.dtype)


def rel_bias(table, dist):
    n = jnp.maximum(dist, 0)
    max_exact = NUM_BUCKETS // 2
    nf = jnp.maximum(n, 1).astype(jnp.float32)
    large = max_exact + (jnp.log(nf / max_exact) / math.log(MAX_DISTANCE / max_exact)
                         * (NUM_BUCKETS - max_exact)).astype(jnp.int32)
    large = jnp.minimum(large, NUM_BUCKETS - 1)
    bucket = jnp.where(n < max_exact, n, large)
    return jnp.take(table, bucket, axis=1).astype(jnp.float32)


def dilated_group_attention(q, k, v, table, window, dilation):
    b, s, h, dh = q.shape
    r = dilation
    n = s // r
    wu = window // dilation
    nb = -(-n // ATTN_BLOCK)
    n_p = nb * ATTN_BLOCK

    def to_blocks(t):
        t = t.reshape(b, n, r, h, dh)
        t = jnp.pad(t, ((0, 0), (0, n_p - n), (0, 0), (0, 0), (0, 0)))
        return t.reshape(b, nb, ATTN_BLOCK, r, h, dh)

    def with_prev(t):
        prev = jnp.pad(t, ((0, 0), (1, 0), (0, 0), (0, 0), (0, 0), (0, 0)))[:, :-1]
        return jnp.concatenate([prev, t], axis=2)

    qb = to_blocks(q)
    kc = with_prev(to_blocks(k))
    vc = with_prev(to_blocks(v))

    qi = jnp.arange(ATTN_BLOCK)[:, None]
    ki = jnp.arange(2 * ATTN_BLOCK)[None, :]
    dist_u = qi + ATTN_BLOCK - ki
    band = (dist_u >= 0) & (dist_u <= wu)
    has_prev = (jnp.arange(nb)[:, None, None] > 0) | (ki[None] >= ATTN_BLOCK)
    mask = band[None] & has_prev
    bias = rel_bias(table, dist_u * r)

    logits = jnp.einsum('bnqrhd,bnkrhd->bnrhqk', qb, kc,
                        preferred_element_type=jnp.float32) * (dh ** -0.5) + bias[None, None, None]
    logits = jnp.where(mask[None, :, None, None], logits, -jnp.inf)
    m = jnp.max(logits, axis=-1, keepdims=True)
    p = jnp.exp(logits - m)
    den = jnp.sum(p, axis=-1, keepdims=True)
    o = jnp.einsum('bnrhqk,bnkrhd->bnqrhd', p / den, vc.astype(jnp.float32))
    lse = (m + jnp.log(den))[..., 0]

    o = o.reshape(b, n_p, r, h, dh)[:, :n].reshape(b, s, h, dh)
    lse = lse.transpose(0, 1, 4, 2, 3).reshape(b, n_p, r, h)[:, :n].reshape(b, s, h)
    return o, lse


def dilated_mixer(x, w_in, w_out, table):
    b, s, _ = x.shape
    proj = (x @ w_in).reshape(b, s, N_GROUPS, 3, N_HEADS, HEAD_DIM)
    outs, lses = [], []
    for g, (window, dil) in enumerate(DIL_GROUPS):
        o, lse = dilated_group_attention(proj[:, :, g, 0], proj[:, :, g, 1], proj[:, :, g, 2],
                                         table, window, dil)
        outs.append(o)
        lses.append(lse)
    alpha = jax.nn.softmax(jnp.stack(lses, axis=-1), axis=-1)
    o = jnp.einsum('gbshd,bshg->bshd', jnp.stack(outs), alpha)
    return o.reshape(b, s, N_HEADS * HEAD_DIM).astype(x.dtype) @ w_out


def diff_mixer(x, k_sh, v_sh, w_q, lq1, lk1, lq2, lk2, subln_g, w_out, table, lambda_init):
    b, s, _ = x.shape
    q = (x @ w_q).reshape(b, s, N_HEADS, 2, HEAD_DIM)
    f32 = jnp.float32
    lam = (jnp.exp(jnp.sum(lq1.astype(f32) * lk1.astype(f32)))
           - jnp.exp(jnp.sum(lq2.astype(f32) * lk2.astype(f32))) + lambda_init)
    nq = s // ATTN_BLOCK
    qb = q.reshape(b, nq, ATTN_BLOCK, N_HEADS, 2, HEAD_DIM).transpose(1, 0, 2, 3, 4, 5)
    kpos = jnp.arange(s)
    scale = HEAD_DIM ** -0.5

    def block(args):
        i, qblk = args
        qpos = i * ATTN_BLOCK + jnp.arange(ATTN_BLOCK)
        dist = qpos[:, None] - kpos[None, :]
        bias = rel_bias(table, dist)
        logits = jnp.einsum('bqhcd,bkhcd->bhcqk', qblk, k_sh,
                            preferred_element_type=f32) * scale + bias[None, :, None]
        logits = jnp.where(dist >= 0, logits, -jnp.inf)
        p = jax.nn.softmax(logits, axis=-1)
        a = p[:, :, 0] - lam * p[:, :, 1]
        return jnp.einsum('bhqk,bkhe->bqhe', a, v_sh.astype(f32))

    o = lax.map(block, (jnp.arange(nq), qb))
    o = o.transpose(1, 0, 2, 3, 4).reshape(b, s, N_HEADS, DIFF_V_DIM)
    o = rms_norm(o, subln_g, SUBLN_EPS) * (1.0 - lambda_init)
    return o.reshape(b, s, N_HEADS * DIFF_V_DIM).astype(x.dtype) @ w_out


def conv_ffn(x, w_up, conv_w, conv_b, w_down):
    s = x.shape[1]
    u = x @ w_up
    up = jnp.pad(u, ((0, 0), (CONV_WIDTH - 1, 0), (0, 0)))
    c = conv_b
    for j in range(CONV_WIDTH):
        c = c + conv_w[j] * up[:, j:j + s]
    gate, val = jnp.split(c, 2, axis=-1)
    return (jax.nn.gelu(gate, approximate=False) * val) @ w_down


def setup_inputs(seed: int = 0) -> dict:
    key = jax.random.key(seed)
    ks = jax.random.split(key, 20)
    f32 = jnp.float32
    D = D_MODEL
    qkv_a = N_GROUPS * 3 * N_HEADS * HEAD_DIM
    a_width = N_HEADS * HEAD_DIM
    qk_b = N_HEADS * 2 * HEAD_DIM
    v_b = N_HEADS * DIFF_V_DIM
    nrm = lambda k, shp: jax.random.normal(k, shp, f32)
    return {
        'x': nrm(ks[0], (BATCH, SEQ, D)),
        'rel_bias_table': 0.2 * nrm(ks[1], (N_HEADS, NUM_BUCKETS)),
        'norm_g': 1.0 + 0.05 * nrm(ks[2], (DEPTH, 4, D)),
        'w_in_a': nrm(ks[3], (N_A_LAYERS, D, qkv_a)) * D ** -0.5,
        'w_out_a': nrm(ks[4], (N_A_LAYERS, a_width, D)) * a_width ** -0.5,
        'kv_norm_g': 1.0 + 0.05 * nrm(ks[5], (D,)),
        'w_k_shared': nrm(ks[6], (D, qk_b)) * D ** -0.5,
        'w_v_shared': nrm(ks[7], (D, v_b)) * D ** -0.5,
        'w_q_b': nrm(ks[8], (N_B_LAYERS, D, qk_b)) * D ** -0.5,
        'lam_q1': 0.1 * nrm(ks[9], (N_B_LAYERS, HEAD_DIM)),
        'lam_k1': 0.1 * nrm(ks[10], (N_B_LAYERS, HEAD_DIM)),
        'lam_q2': 0.1 * nrm(ks[11], (N_B_LAYERS, HEAD_DIM)),
        'lam_k2': 0.1 * nrm(ks[12], (N_B_LAYERS, HEAD_DIM)),
        'subln_g': 1.0 + 0.05 * nrm(ks[13], (N_B_LAYERS, DIFF_V_DIM)),
        'w_out_b': nrm(ks[14], (N_B_LAYERS, v_b, D)) * v_b ** -0.5,
        'w_up': nrm(ks[15], (DEPTH, D, 2 * D_FF)) * D ** -0.5,
        'conv_w': nrm(ks[16], (DEPTH, CONV_WIDTH, 2 * D_FF)) * CONV_WIDTH ** -0.5,
        'conv_b': 0.02 * nrm(ks[17], (DEPTH, 2 * D_FF)),
        'w_down': nrm(ks[18], (DEPTH, D_FF, D)) * D_FF ** -0.5,
    }


def reference(x, rel_bias_table, norm_g, w_in_a, w_out_a, kv_norm_g, w_k_shared, w_v_shared,
              w_q_b, lam_q1, lam_k1, lam_q2, lam_k2, subln_g, w_out_b, w_up, conv_w, conv_b,
              w_down):
    b, s, _ = x.shape
    h = x
    k_shared = None
    v_shared = None
    for layer in range(DEPTH):
        g = norm_g[layer]
        if layer < N_A_LAYERS:
            mix = dilated_mixer(rms_norm(h, g[0]), w_in_a[layer], w_out_a[layer], rel_bias_table)
        else:
            j = layer - N_A_LAYERS
            lambda_init = 0.8 - 0.6 * math.exp(-0.3 * layer)
            mix = diff_mixer(rms_norm(h, g[0]), k_shared, v_shared, w_q_b[j],
                             lam_q1[j], lam_k1[j], lam_q2[j], lam_k2[j], subln_g[j],
                             w_out_b[j], rel_bias_table, lambda_init)
        h = h + rms_norm(mix, g[1])
        ff = conv_ffn(rms_norm(h, g[2]), w_up[layer], conv_w[layer], conv_b[layer], w_down[layer])
        h = h + rms_norm(ff, g[3])
        if layer == N_A_LAYERS - 1:
            kv_src = rms_norm(h, kv_norm_g)
            k_shared = (kv_src @ w_k_shared).reshape(b, s, N_HEADS, 2, HEAD_DIM)
            v_shared = (kv_src @ w_v_shared).reshape(b, s, N_HEADS, DIFF_V_DIM)
    return h
```

```python
import functools
import math

import numpy as np
import jax
import jax.numpy as jnp
from jax import lax
from jax.experimental import pallas as pl
from jax.experimental.pallas import tpu as pltpu

F32 = jnp.float32
BF16 = jnp.bfloat16

HEAD_DIM = 64
N_HEADS = 8
DIL_GROUPS = ((128, 1), (512, 4), (2048, 16))
ATTN_BLOCK = 128
NUM_BUCKETS = 32
MAX_DISTANCE = 2048
CONV_WIDTH = 3
RMS_EPS = 1e-6
SUBLN_EPS = 1e-5

LANES = 128
VMEM_LIMIT = 56 * 1024 * 1024

NEG_INF = float("-inf")


def _cparams(sem):
    return pltpu.CompilerParams(dimension_semantics=sem, vmem_limit_bytes=VMEM_LIMIT)


def _bucket_np(dist):
    n = np.maximum(dist, 0)
    max_exact = NUM_BUCKETS // 2

    def large(dtype):
        nf = np.maximum(n, 1).astype(dtype)
        v = np.log(nf / dtype(max_exact)) / dtype(math.log(MAX_DISTANCE / max_exact)) * dtype(NUM_BUCKETS - max_exact)
        return np.minimum(max_exact + v.astype(np.int32), NUM_BUCKETS - 1)

    l32, l64 = large(np.float32), large(np.float64)
    assert np.array_equal(l32, l64), "bucket boundary is rounding sensitive"
    return np.where(n < max_exact, n, l32).astype(np.int32)


def _bias_kernel(tbl_ref, idx_ref, o_ref):
    h = pl.program_id(1)
    idx = idx_ref[...]
    acc = jnp.full(idx.shape, NEG_INF, F32)
    for k in range(NUM_BUCKETS):
        acc = jnp.where(idx == k, tbl_ref[h, k], acc)
    o_ref[...] = acc


def _bias_tiles(table, idx, head_major):
    t, r, c = idx.shape
    nh = table.shape[0]
    if head_major:
        out_shape = (nh, t, r, c)
        out_map = lambda i, h: (h, i, 0, 0)
    else:
        out_shape = (t, nh, r, c)
        out_map = lambda i, h: (i, h, 0, 0)
    return pl.pallas_call(
        _bias_kernel,
        grid=(t, nh),
        in_specs=[pl.BlockSpec(memory_space=pltpu.SMEM),
                  pl.BlockSpec((None, r, c), lambda i, h: (i, 0, 0))],
        out_specs=pl.BlockSpec((None, None, r, c), out_map),
        out_shape=jax.ShapeDtypeStruct(out_shape, F32),
        compiler_params=_cparams(("arbitrary", "arbitrary")),
        name="rel_bias_tiles",
    )(table, jnp.asarray(idx))


def _dilated_bias_idx():
    qi = np.arange(ATTN_BLOCK)[:, None]
    ki = np.arange(2 * ATTN_BLOCK)[None, :]
    dist_u = qi + ATTN_BLOCK - ki
    out = []
    for window, dil in DIL_GROUPS:
        band = (dist_u >= 0) & (dist_u <= window // dil)
        out.append(np.where(band, _bucket_np(dist_u * dil), -1))
    return np.stack(out).astype(np.int32)


def _diff_bias_idx(seq, tile):
    nq = seq // tile
    i = np.arange(tile)[:, None]
    j = np.arange(tile)[None, :]
    tiles = []
    for d in range(nq):
        dist = d * tile + i - j
        tiles.append(np.where(dist >= 0, _bucket_np(dist), -1).astype(np.int32))
    nd = nq
    while nd > 1 and np.array_equal(tiles[nd - 1], tiles[nd - 2]):
        nd -= 1
    assert all(np.array_equal(tiles[d], tiles[nd - 1]) for d in range(nd - 1, nq))
    return np.stack(tiles[:nd])


def _rms(x, g, eps):
    return x * lax.rsqrt(jnp.mean(x * x, axis=-1, keepdims=True) + eps) * g


def _norm_matmul_kernel(x_ref, g_ref, w_ref, o_ref, *, n_chunk):
    xn = _rms(x_ref[...], g_ref[...], RMS_EPS).astype(BF16)
    n = o_ref.shape[-1]
    for j in range(n // n_chunk):
        sl = slice(j * n_chunk, (j + 1) * n_chunk)
        o_ref[:, sl] = jnp.dot(xn, w_ref[:, sl], preferred_element_type=F32).astype(o_ref.dtype)


def _norm_matmul(x2d, g, w_bf16, tm=512, n_chunk=512):
    m, d = x2d.shape
    n = w_bf16.shape[1]
    return pl.pallas_call(
        functools.partial(_norm_matmul_kernel, n_chunk=n_chunk),
        grid=(m // tm,),
        in_specs=[pl.BlockSpec((tm, d), lambda i: (i, 0)),
                  pl.BlockSpec((1, d), lambda i: (0, 0)),
                  pl.BlockSpec((d, n), lambda i: (0, 0))],
        out_specs=pl.BlockSpec((tm, n), lambda i: (i, 0)),
        out_shape=jax.ShapeDtypeStruct((m, n), BF16),
        compiler_params=_cparams(("arbitrary",)),
        name="norm_matmul",
    )(x2d, g.reshape(1, d), w_bf16)


def _dil_kernel(q_ref, k_ref, v_ref, kp_ref, vp_ref, bias_ref, o_ref, lse_ref, *, tq):
    blk = ATTN_BLOCK
    i = pl.program_id(2)
    lane = lax.broadcasted_iota(jnp.int32, (1, LANES), 1)
    col = lax.broadcasted_iota(jnp.int32, (blk, 2 * blk), 1)
    first_pen = jnp.where(jnp.logical_and(i == 0, col < blk), NEG_INF, 0.0).astype(F32)
    scale = HEAD_DIM ** -0.5
    for j in range(tq // blk):
        rows = slice(j * blk, (j + 1) * blk)
        qj = q_ref[rows, :]
        if j == 0:
            kx = jnp.concatenate([kp_ref[...], k_ref[0:blk, :]], axis=0)
            vx = jnp.concatenate([vp_ref[...], v_ref[0:blk, :]], axis=0)
        else:
            kx = k_ref[(j - 1) * blk:(j + 1) * blk, :]
            vx = v_ref[(j - 1) * blk:(j + 1) * blk, :]
        lse_tile = jnp.zeros((blk, LANES), F32)
        for hp in range(N_HEADS // 2):
            cols = slice(hp * LANES, (hp + 1) * LANES)
            q2, k2, v2 = qj[:, cols], kx[:, cols], vx[:, cols]
            o_pair = None
            for e in range(2):
                h = 2 * hp + e
                sel = (lane >= HEAD_DIM) if e else (lane < HEAD_DIM)
                qm = jnp.where(sel, q2, jnp.zeros_like(q2))
                s = lax.dot_general(qm, k2, (((1,), (1,)), ((), ())), preferred_element_type=F32)
                s = s * scale + bias_ref[h]
                if j == 0:
                    s = s + first_pen
                m = jnp.max(s, axis=-1, keepdims=True)
                p = jnp.exp(s - m)
                l = jnp.sum(p, axis=-1, keepdims=True)
                pv = jnp.dot(p.astype(BF16), v2, preferred_element_type=F32)
                o_e = pv * (1.0 / l)
                o_pair = o_e if e == 0 else jnp.where(lane < HEAD_DIM, o_pair, o_e)
                lse_tile = jnp.where(lane == h, m + jnp.log(l), lse_tile)
            o_ref[rows, cols] = o_pair
        lse_ref[rows, :] = lse_tile


def _dilated_group(proj, bias, g, dil):
    b, s, pw = proj.shape
    w = N_HEADS * HEAD_DIM
    n = s // dil
    assert n * dil == s and n % ATTN_BLOCK == 0
    tq = min(n, 2 * ATTN_BLOCK)
    sub = tq // ATTN_BLOCK
    pv = proj.reshape(b, n, dil * pw)
    nblk = pw // w
    base = g * 3

    def cur(which):
        return pl.BlockSpec((None, tq, w), lambda bb, c, i: (bb, i, c * nblk + base + which))

    def prev(which):
        return pl.BlockSpec((None, ATTN_BLOCK, w),
                            lambda bb, c, i: (bb, jnp.maximum(i * sub - 1, 0), c * nblk + base + which))

    o, lse = pl.pallas_call(
        functools.partial(_dil_kernel, tq=tq),
        grid=(b, dil, n // tq),
        in_specs=[cur(0), cur(1), cur(2), prev(1), prev(2),
                  pl.BlockSpec((None, N_HEADS, ATTN_BLOCK, 2 * ATTN_BLOCK), lambda bb, c, i: (g, 0, 0, 0))],
        out_specs=[pl.BlockSpec((None, tq, w), lambda bb, c, i: (bb, i, c)),
                   pl.BlockSpec((None, tq, LANES), lambda bb, c, i: (bb, i, c))],
        out_shape=[jax.ShapeDtypeStruct((b, n, dil * w), F32),
                   jax.ShapeDtypeStruct((b, n, dil * LANES), F32)],
        compiler_params=_cparams(("arbitrary", "arbitrary", "arbitrary")),
        name=f"dilated_attn_g{g}",
    )(pv, pv, pv, pv, pv, bias)
    return o.reshape(b * s, w), lse.reshape(b * s, LANES)


def _combine_out_kernel(o0, o1, o2, l0, l1, l2, e_ref, w_ref, g_ref, h_ref, out_ref):
    ls = [l0[...], l1[...], l2[...]]
    mx = jnp.maximum(jnp.maximum(ls[0], ls[1]), ls[2])
    ws = [jnp.exp(l - mx) for l in ls]
    inv = 1.0 / (ws[0] + ws[1] + ws[2])
    a = None
    for og, wg in zip((o0, o1, o2), ws):
        ax = jnp.dot(wg * inv, e_ref[...], preferred_element_type=F32, precision=lax.Precision.HIGHEST)
        t = ax * og[...]
        a = t if a is None else a + t
    y = jnp.dot(a.astype(BF16), w_ref[...], preferred_element_type=F32)
    out_ref[...] = h_ref[...] + _rms(y, g_ref[...], RMS_EPS)


def _combine_out(os_, lses, w_bf16, g, h2d, tm=512):
    m, d = h2d.shape
    w = os_[0].shape[1]
    expand = np.zeros((LANES, w), np.float32)
    for hh in range(N_HEADS):
        expand[hh, hh * HEAD_DIM:(hh + 1) * HEAD_DIM] = 1.0
    row = lambda width: pl.BlockSpec((tm, width), lambda i: (i, 0))
    full = lambda shp: pl.BlockSpec(shp, lambda i: (0, 0))
    return pl.pallas_call(
        _combine_out_kernel,
        grid=(m // tm,),
        in_specs=[row(w)] * 3 + [row(LANES)] * 3 + [full((LANES, w)), full((w, d)), full((1, d)), row(d)],
        out_specs=row(d),
        out_shape=jax.ShapeDtypeStruct((m, d), F32),
        compiler_params=_cparams(("arbitrary",)),
        name="combine_out_proj",
    )(*os_, *lses, jnp.asarray(expand), w_bf16, g.reshape(1, d), h2d)


def _out_kernel(a_ref, w_ref, g_ref, h_ref, out_ref):
    y = jnp.dot(a_ref[...], w_ref[...], preferred_element_type=F32)
    out_ref[...] = h_ref[...] + _rms(y, g_ref[...], RMS_EPS)


def _out_proj(a2d, w_bf16, g, h2d, tm=512):
    m, d = h2d.shape
    k = a2d.shape[1]
    return pl.pallas_call(
        _out_kernel,
        grid=(m // tm,),
        in_specs=[pl.BlockSpec((tm, k), lambda i: (i, 0)),
                  pl.BlockSpec((k, d), lambda i: (0, 0)),
                  pl.BlockSpec((1, d), lambda i: (0, 0)),
                  pl.BlockSpec((tm, d), lambda i: (i, 0))],
        out_specs=pl.BlockSpec((tm, d), lambda i: (i, 0)),
        out_shape=jax.ShapeDtypeStruct((m, d), F32),
        compiler_params=_cparams(("arbitrary",)),
        name="out_proj",
    )(a2d, w_bf16, g.reshape(1, d), h2d)


def _ffn_kernel(h_ref, g_in_ref, wup_ref, cw_ref, cb_ref, wdn_ref, g_out_ref, out_ref,
                halo_ref, acc_ref, *, tiles_per_seq, f_chunk):
    tm = h_ref.shape[0]
    d_ff = wdn_ref.shape[0]
    i = pl.program_id(0)
    h = h_ref[...]
    xn = _rms(h, g_in_ref[...], RMS_EPS).astype(BF16)
    row = lax.broadcasted_iota(jnp.int32, (tm, 1), 0)
    seq_start = (i % tiles_per_seq) == 0
    inv_sqrt2 = 1.0 / math.sqrt(2.0)

    def conv(u, cols):
        prev = jnp.where(seq_start, 0.0, halo_ref[:, cols])
        halo_ref[:, cols] = u[tm - 8:tm, :]
        u1 = pltpu.roll(u, 1, 0)
        u1 = jnp.where(row == 0, prev[7:8, :], u1)
        u2 = pltpu.roll(u, 2, 0)
        u2 = jnp.where(row == 0, prev[6:7, :], jnp.where(row == 1, prev[7:8, :], u2))
        return (cb_ref[:, cols] + cw_ref[0:1, cols] * u2 + cw_ref[1:2, cols] * u1 + cw_ref[2:3, cols] * u)

    for c in range(d_ff // f_chunk):
        gcols = slice(c * f_chunk, (c + 1) * f_chunk)
        vcols = slice(d_ff + c * f_chunk, d_ff + (c + 1) * f_chunk)
        gate = conv(jnp.dot(xn, wup_ref[:, gcols], preferred_element_type=F32), gcols)
        val = conv(jnp.dot(xn, wup_ref[:, vcols], preferred_element_type=F32), vcols)
        act = (0.5 * gate * (1.0 + lax.erf(gate * inv_sqrt2)) * val).astype(BF16)
        part = jnp.dot(act, wdn_ref[gcols, :], preferred_element_type=F32)
        if c == 0:
            acc_ref[...] = part
        else:
            acc_ref[...] += part
    out_ref[...] = h + _rms(acc_ref[...], g_out_ref[...], RMS_EPS)


def _conv_ffn(h2d, g_in, w_up_bf16, conv_w, conv_b, w_down_bf16, g_out, seq, tm=512, f_chunk=256):
    m, d = h2d.shape
    f2 = w_up_bf16.shape[1]
    d_ff = f2 // 2
    assert seq % tm == 0 and d_ff % f_chunk == 0
    full = lambda shp: pl.BlockSpec(shp, lambda i: (0, 0))
    return pl.pallas_call(
        functools.partial(_ffn_kernel, tiles_per_seq=seq // tm, f_chunk=f_chunk),
        grid=(m // tm,),
        in_specs=[pl.BlockSpec((tm, d), lambda i: (i, 0)),
                  full((1, d)), full((d, f2)), full((CONV_WIDTH, f2)), full((1, f2)),
                  full((d_ff, d)), full((1, d))],
        out_specs=pl.BlockSpec((tm, d), lambda i: (i, 0)),
        out_shape=jax.ShapeDtypeStruct((m, d), F32),
        scratch_shapes=[pltpu.VMEM((8, f2), F32), pltpu.VMEM((tm, d), F32)],
        compiler_params=_cparams(("arbitrary",)),
        name="conv_ffn",
    )(h2d, g_in.reshape(1, d), w_up_bf16, conv_w, conv_b.reshape(1, f2), w_down_bf16, g_out.reshape(1, d))


def _diff_kernel(q_ref, k_ref, v_ref, bias_ref, lq1, lk1, lq2, lk2, sg_ref, o_ref,
                 m_sc, l_sc, acc_sc, *, tile, n_bias, lambda_init):
    qi = pl.program_id(2)
    lane = lax.broadcasted_iota(jnp.int32, (1, LANES), 1)
    q = q_ref[...]
    zero = jnp.zeros_like(q)
    scale = jnp.asarray(HEAD_DIM ** -0.5, q.dtype)
    qs = jnp.concatenate([jnp.where(lane < HEAD_DIM, q, zero), jnp.where(lane >= HEAD_DIM, q, zero)], axis=0) * scale

    m_sc[...] = jnp.full(m_sc.shape, NEG_INF, F32)
    l_sc[...] = jnp.zeros(l_sc.shape, F32)
    acc_sc[...] = jnp.zeros(acc_sc.shape, F32)

    def body(ki, carry):
        start = pl.multiple_of(ki * tile, tile)
        k = k_ref[pl.ds(start, tile), :]
        v = v_ref[pl.ds(start, tile), :]
        s = lax.dot_general(qs, k, (((1,), (1,)), ((), ())), preferred_element_type=F32)
        bias = bias_ref[jnp.minimum(qi - ki, n_bias - 1)]
        s = s + jnp.concatenate([bias, bias], axis=0)
        m_prev = m_sc[...]
        m_new = jnp.maximum(m_prev, jnp.max(s, axis=-1, keepdims=True))
        alpha = jnp.exp(m_prev - m_new)
        p = jnp.exp(s - jnp.concatenate([m_new] * (tile // LANES), axis=1))
        l_sc[...] = alpha * l_sc[...] + jnp.sum(p, axis=-1, keepdims=True)
        acc_sc[...] = alpha * acc_sc[...] + jnp.dot(p.astype(BF16), v, preferred_element_type=F32)
        m_sc[...] = m_new
        return carry

    lax.fori_loop(0, qi + 1, body, 0)

    f32 = F32
    lam = (jnp.exp(jnp.sum(lq1[...].astype(f32) * lk1[...].astype(f32), axis=-1, keepdims=True))
           - jnp.exp(jnp.sum(lq2[...].astype(f32) * lk2[...].astype(f32), axis=-1, keepdims=True))
           + lambda_init)
    o = acc_sc[...] * (1.0 / l_sc[...])
    a = o[0:tile, :] - lam * o[tile:2 * tile, :]
    y = _rms(a, sg_ref[...], SUBLN_EPS) * (1.0 - lambda_init)
    o_ref[...] = y.astype(o_ref.dtype)


def _diff_attention(q, kv, bias, lq1, lk1, lq2, lk2, subln_g, lambda_init, tile):
    b, s, _ = q.shape
    n_bias = bias.shape[1]
    hd2 = 2 * HEAD_DIM
    vec = lambda a: a.reshape(1, -1)
    small = lambda n: pl.BlockSpec((1, n), lambda h, bb, i: (0, 0))
    return pl.pallas_call(
        functools.partial(_diff_kernel, tile=tile, n_bias=n_bias, lambda_init=lambda_init),
        grid=(N_HEADS, b, s // tile),
        in_specs=[pl.BlockSpec((None, tile, hd2), lambda h, bb, i: (bb, i, h)),
                  pl.BlockSpec((None, s, hd2), lambda h, bb, i: (bb, 0, h)),
                  pl.BlockSpec((None, s, hd2), lambda h, bb, i: (bb, 0, N_HEADS + h)),
                  pl.BlockSpec((None, n_bias, tile, tile), lambda h, bb, i: (h, 0, 0, 0)),
                  small(HEAD_DIM), small(HEAD_DIM), small(HEAD_DIM), small(HEAD_DIM), small(hd2)],
        out_specs=pl.BlockSpec((None, tile, hd2), lambda h, bb, i: (bb, i, h)),
        out_shape=jax.ShapeDtypeStruct((b, s, N_HEADS * hd2), BF16),
        scratch_shapes=[pltpu.VMEM((2 * tile, LANES), F32), pltpu.VMEM((2 * tile, LANES), F32),
                        pltpu.VMEM((2 * tile, hd2), F32)],
        compiler_params=_cparams(("arbitrary", "arbitrary", "arbitrary")),
        name="diff_attn",
    )(q, kv, kv, bias, vec(lq1), vec(lk1), vec(lq2), vec(lk2), vec(subln_g))


DIFF_TILE = 256


def kernel(x, rel_bias_table, norm_g, w_in_a, w_out_a, kv_norm_g, w_k_shared, w_v_shared, w_q_b,
           lam_q1, lam_k1, lam_q2, lam_k2, subln_g, w_out_b, w_up, conv_w, conv_b, w_down):
    b, s, d = x.shape
    depth = norm_g.shape[0]
    n_a = w_in_a.shape[0]
    bf = lambda a: a.astype(BF16)

    dil_bias = _bias_tiles(rel_bias_table, _dilated_bias_idx(), head_major=False)
    diff_tile = min(DIFF_TILE, s)
    diff_bias = _bias_tiles(rel_bias_table, _diff_bias_idx(s, diff_tile), head_major=True)

    h = x.reshape(b * s, d)
    kv = None
    for layer in range(depth):
        g = norm_g[layer]
        if layer < n_a:
            proj = _norm_matmul(h, g[0], bf(w_in_a[layer])).reshape(b, s, -1)
            outs = [_dilated_group(proj, dil_bias, gi, dil) for gi, (_, dil) in enumerate(DIL_GROUPS)]
            h = _combine_out([o for o, _ in outs], [l for _, l in outs], bf(w_out_a[layer]), g[1], h)
        else:
            j = layer - n_a
            lambda_init = 0.8 - 0.6 * math.exp(-0.3 * layer)
            q = _norm_matmul(h, g[0], bf(w_q_b[j])).reshape(b, s, -1)
            a = _diff_attention(q, kv, diff_bias, lam_q1[j], lam_k1[j], lam_q2[j], lam_k2[j],
                                subln_g[j], lambda_init, diff_tile)
            h = _out_proj(a.reshape(b * s, -1), bf(w_out_b[j]), g[1], h)
        h = _conv_ffn(h, g[2], bf(w_up[layer]), conv_w[layer], conv_b[layer], bf(w_down[layer]), g[3], s)
        if layer == n_a - 1:
            w_kv = jnp.concatenate([w_k_shared, w_v_shared], axis=1)
            kv = _norm_matmul(h, kv_norm_g, bf(w_kv)).reshape(b, s, -1)
    return h.reshape(b, s, d)
```

```python
import functools
import math

import numpy as np
import jax
import jax.numpy as jnp
from jax import lax
from jax.experimental import pallas as pl
from jax.experimental.pallas import tpu as pltpu

F32 = jnp.float32
BF16 = jnp.bfloat16

HEAD_DIM = 64
N_HEADS = 8
DIL_GROUPS = ((128, 1), (512, 4), (2048, 16))
ATTN_BLOCK = 128
NUM_BUCKETS = 32
MAX_DISTANCE = 2048
CONV_WIDTH = 3
RMS_EPS = 1e-6
SUBLN_EPS = 1e-5

LANES = 128
VMEM_LIMIT = 56 * 1024 * 1024

NEG_INF = float("-inf")


def _cparams(sem):
    return pltpu.CompilerParams(dimension_semantics=sem, vmem_limit_bytes=VMEM_LIMIT)


def _bucket_np(dist):
    n = np.maximum(dist, 0)
    max_exact = NUM_BUCKETS // 2

    def large(dtype):
        nf = np.maximum(n, 1).astype(dtype)
        v = np.log(nf / dtype(max_exact)) / dtype(math.log(MAX_DISTANCE / max_exact)) * dtype(NUM_BUCKETS - max_exact)
        return np.minimum(max_exact + v.astype(np.int32), NUM_BUCKETS - 1)

    l32, l64 = large(np.float32), large(np.float64)
    assert np.array_equal(l32, l64), "bucket boundary is rounding sensitive"
    return np.where(n < max_exact, n, l32).astype(np.int32)


def _bias_kernel(tbl_ref, idx_ref, o_ref):
    h = pl.program_id(1)
    idx = idx_ref[...]
    acc = jnp.full(idx.shape, NEG_INF, F32)
    for k in range(NUM_BUCKETS):
        acc = jnp.where(idx == k, tbl_ref[h, k], acc)
    o_ref[...] = acc


def _bias_tiles(table, idx, head_major):
    t, r, c = idx.shape
    nh = table.shape[0]
    if head_major:
        out_shape = (nh, t, r, c)
        out_map = lambda i, h: (h, i, 0, 0)
    else:
        out_shape = (t, nh, r, c)
        out_map = lambda i, h: (i, h, 0, 0)
    return pl.pallas_call(
        _bias_kernel,
        grid=(t, nh),
        in_specs=[pl.BlockSpec(memory_space=pltpu.SMEM),
                  pl.BlockSpec((None, r, c), lambda i, h: (i, 0, 0))],
        out_specs=pl.BlockSpec((None, None, r, c), out_map),
        out_shape=jax.ShapeDtypeStruct(out_shape, F32),
        compiler_params=_cparams(("arbitrary", "arbitrary")),
        name="rel_bias_tiles",
    )(table, jnp.asarray(idx))


def _dilated_bias_idx():
    qi = np.arange(ATTN_BLOCK)[:, None]
    ki = np.arange(2 * ATTN_BLOCK)[None, :]
    dist_u = qi + ATTN_BLOCK - ki
    out = []
    for window, dil in DIL_GROUPS:
        band = (dist_u >= 0) & (dist_u <= window // dil)
        out.append(np.where(band, _bucket_np(dist_u * dil), -1))
    return np.stack(out).astype(np.int32)


def _diff_bias_idx(seq, tile):
    nq = seq // tile
    i = np.arange(tile)[:, None]
    j = np.arange(tile)[None, :]
    tiles = []
    for d in range(nq):
        dist = d * tile + i - j
        tiles.append(np.where(dist >= 0, _bucket_np(dist), -1).astype(np.int32))
    nd = nq
    while nd > 1 and np.array_equal(tiles[nd - 1], tiles[nd - 2]):
        nd -= 1
    assert all(np.array_equal(tiles[d], tiles[nd - 1]) for d in range(nd - 1, nq))
    return np.stack(tiles[:nd])


def _rms(x, g, eps):
    return x * lax.rsqrt(jnp.mean(x * x, axis=-1, keepdims=True) + eps) * g


def _norm_matmul_kernel(x_ref, g_ref, w_ref, o_ref, *, n_chunk):
    xn = _rms(x_ref[...], g_ref[...], RMS_EPS).astype(BF16)
    n = o_ref.shape[-1]
    for j in range(n // n_chunk):
        sl = slice(j * n_chunk, (j + 1) * n_chunk)
        o_ref[:, sl] = jnp.dot(xn, w_ref[:, sl], preferred_element_type=F32).astype(o_ref.dtype)


def _norm_matmul(x2d, g, w_bf16, tm=512, n_chunk=512):
    m, d = x2d.shape
    n = w_bf16.shape[1]
    return pl.pallas_call(
        functools.partial(_norm_matmul_kernel, n_chunk=n_chunk),
        grid=(m // tm,),
        in_specs=[pl.BlockSpec((tm, d), lambda i: (i, 0)),
                  pl.BlockSpec((1, d), lambda i: (0, 0)),
                  pl.BlockSpec((d, n), lambda i: (0, 0))],
        out_specs=pl.BlockSpec((tm, n), lambda i: (i, 0)),
        out_shape=jax.ShapeDtypeStruct((m, n), BF16),
        compiler_params=_cparams(("arbitrary",)),
        name="norm_matmul",
    )(x2d, g.reshape(1, d), w_bf16)


def _proj_a_kernel(x_ref, g_ref, w_ref, *refs, n_chunk):
    out_refs, xn_sc = refs[:-1], refs[-1]
    tm = x_ref.shape[0]
    xn = _rms(x_ref[...], g_ref[...], RMS_EPS)
    nlb = xn.shape[1] // LANES
    for j in range(nlb):
        xn_sc[j] = xn[:, j * LANES:(j + 1) * LANES]
    gw = out_refs[0].shape[-1]
    for gi, (o_ref, (_, dil)) in enumerate(zip(out_refs, DIL_GROUPS)):
        per = tm // dil
        if dil == 1:
            xg = xn
        else:
            xg = jnp.concatenate(
                [jnp.concatenate([xn_sc[j, pl.ds(c, per, stride=dil), :] for c in range(dil)], axis=0)
                 for j in range(nlb)], axis=1)
        xg = xg.astype(BF16)
        for j in range(gw // n_chunk):
            res = jnp.dot(xg, w_ref[:, gi * gw + j * n_chunk:gi * gw + (j + 1) * n_chunk],
                          preferred_element_type=F32).astype(o_ref.dtype)
            for c in range(dil):
                o_ref[c, :, j * n_chunk:(j + 1) * n_chunk] = res[c * per:(c + 1) * per, :]


def _proj_a(x2d, g, w_bf16, b, s, tm=512, n_chunk=512):
    m, d = x2d.shape
    gw = w_bf16.shape[1] // len(DIL_GROUPS)
    tps = s // tm
    assert all(tm % (16 * dil) == 0 for _, dil in DIL_GROUPS)
    return pl.pallas_call(
        functools.partial(_proj_a_kernel, n_chunk=n_chunk),
        grid=(b, tps),
        in_specs=[pl.BlockSpec((tm, d), lambda bb, i: (bb * tps + i, 0)),
                  pl.BlockSpec((1, d), lambda bb, i: (0, 0)),
                  pl.BlockSpec(w_bf16.shape, lambda bb, i: (0, 0))],
        out_specs=[pl.BlockSpec((None, dil, tm // dil, gw), lambda bb, i: (bb, 0, i, 0)) for _, dil in DIL_GROUPS],
        out_shape=[jax.ShapeDtypeStruct((b, dil, s // dil, gw), BF16) for _, dil in DIL_GROUPS],
        scratch_shapes=[pltpu.VMEM((d // LANES, tm, LANES), F32)],
        compiler_params=_cparams(("arbitrary", "arbitrary")),
        name="proj_dilated",
    )(x2d, g.reshape(1, d), w_bf16)


def _dil_kernel(q_ref, k_ref, v_ref, kp_ref, vp_ref, bias_ref, o_ref, lse_ref, *, tq):
    blk = ATTN_BLOCK
    i = pl.program_id(2)
    lane = lax.broadcasted_iota(jnp.int32, (1, LANES), 1)
    col = lax.broadcasted_iota(jnp.int32, (blk, 2 * blk), 1)
    first_pen = jnp.where(jnp.logical_and(i == 0, col < blk), NEG_INF, 0.0).astype(F32)
    scale = HEAD_DIM ** -0.5
    for j in range(tq // blk):
        rows = slice(j * blk, (j + 1) * blk)
        qj = q_ref[rows, :]
        if j == 0:
            kx = jnp.concatenate([kp_ref[...], k_ref[0:blk, :]], axis=0)
            vx = jnp.concatenate([vp_ref[...], v_ref[0:blk, :]], axis=0)
        else:
            kx = k_ref[(j - 1) * blk:(j + 1) * blk, :]
            vx = v_ref[(j - 1) * blk:(j + 1) * blk, :]
        lse_tile = jnp.zeros((blk, LANES), F32)
        for hp in range(N_HEADS // 2):
            cols = slice(hp * LANES, (hp + 1) * LANES)
            q2, k2, v2 = qj[:, cols], kx[:, cols], vx[:, cols]
            o_pair = None
            for e in range(2):
                h = 2 * hp + e
                sel = (lane >= HEAD_DIM) if e else (lane < HEAD_DIM)
                qm = jnp.where(sel, q2, jnp.zeros_like(q2))
                s = lax.dot_general(qm, k2, (((1,), (1,)), ((), ())), preferred_element_type=F32)
                s = s * scale + bias_ref[h]
                if j == 0:
                    s = s + first_pen
                m = jnp.max(s, axis=-1, keepdims=True)
                p = jnp.exp(s - m)
                l = jnp.sum(p, axis=-1, keepdims=True)
                pv = jnp.dot(p.astype(BF16), v2, preferred_element_type=F32)
                o_e = pv * (1.0 / l)
                o_pair = o_e if e == 0 else jnp.where(lane < HEAD_DIM, o_pair, o_e)
                lse_tile = jnp.where(lane == h, m + jnp.log(l), lse_tile)
            o_ref[rows, cols] = o_pair
        lse_ref[rows, :] = lse_tile


def _dilated_group(qkv, bias, g, dil):
    b, r, n, _ = qkv.shape
    w = N_HEADS * HEAD_DIM
    assert r == dil and n % ATTN_BLOCK == 0
    tq = min(n, 2 * ATTN_BLOCK)
    sub = tq // ATTN_BLOCK

    def cur(which):
        return pl.BlockSpec((None, None, tq, w), lambda bb, c, i: (bb, c, i, which))

    def prev(which):
        return pl.BlockSpec((None, None, ATTN_BLOCK, w),
                            lambda bb, c, i: (bb, c, jnp.maximum(i * sub - 1, 0), which))

    return pl.pallas_call(
        functools.partial(_dil_kernel, tq=tq),
        grid=(b, dil, n // tq),
        in_specs=[cur(0), cur(1), cur(2), prev(1), prev(2),
                  pl.BlockSpec((None, N_HEADS, ATTN_BLOCK, 2 * ATTN_BLOCK), lambda bb, c, i: (g, 0, 0, 0))],
        out_specs=[pl.BlockSpec((None, None, tq, w), lambda bb, c, i: (bb, c, i, 0)),
                   pl.BlockSpec((None, None, tq, LANES), lambda bb, c, i: (bb, c, i, 0))],
        out_shape=[jax.ShapeDtypeStruct((b, dil, n, w), F32),
                   jax.ShapeDtypeStruct((b, dil, n, LANES), F32)],
        compiler_params=_cparams(("arbitrary", "arbitrary", "arbitrary")),
        name=f"dilated_attn_g{g}",
    )(qkv, qkv, qkv, qkv, qkv, bias)


def _combine_out_kernel(o0, o1, o2, l0, l1, l2, e_ref, w_ref, g_ref, h_ref, out_ref, o_sc, l_sc):
    tm = h_ref.shape[0]

    def token_order(o_ref, l_ref, slot):
        dil = o_ref.shape[0]
        if dil == 1:
            return o_ref[0], l_ref[0]
        per = tm // dil
        nlb = o_ref.shape[-1] // LANES
        for c in range(dil):
            oc = o_ref[c]
            for j in range(nlb):
                o_sc[slot, j, pl.ds(c, per, stride=dil), :] = oc[:, j * LANES:(j + 1) * LANES]
            l_sc[slot, pl.ds(c, per, stride=dil), :] = l_ref[c]
        return jnp.concatenate([o_sc[slot, j] for j in range(nlb)], axis=1), l_sc[slot]

    pairs = [token_order(o_ref, l_ref, slot) for slot, (o_ref, l_ref) in enumerate(((o0, l0), (o1, l1), (o2, l2)))]
    ls = [l for _, l in pairs]
    mx = jnp.maximum(jnp.maximum(ls[0], ls[1]), ls[2])
    ws = [jnp.exp(l - mx) for l in ls]
    inv = 1.0 / (ws[0] + ws[1] + ws[2])
    a = None
    for (og, _), wg in zip(pairs, ws):
        ax = jnp.dot(wg * inv, e_ref[...], preferred_element_type=F32, precision=lax.Precision.HIGHEST)
        t = ax * og
        a = t if a is None else a + t
    y = jnp.dot(a.astype(BF16), w_ref[...], preferred_element_type=F32)
    out_ref[...] = h_ref[...] + _rms(y, g_ref[...], RMS_EPS)


def _combine_out(os_, lses, w_bf16, g, h2d, b, s, tm=512):
    m, d = h2d.shape
    w = os_[0].shape[-1]
    tps = s // tm
    expand = np.zeros((LANES, w), np.float32)
    for hh in range(N_HEADS):
        expand[hh, hh * HEAD_DIM:(hh + 1) * HEAD_DIM] = 1.0
    grp = lambda a: pl.BlockSpec((None, a.shape[1], tm // a.shape[1], a.shape[3]), lambda bb, i: (bb, 0, i, 0))
    row = pl.BlockSpec((tm, d), lambda bb, i: (bb * tps + i, 0))
    full = lambda shp: pl.BlockSpec(shp, lambda bb, i: (0, 0))
    return pl.pallas_call(
        _combine_out_kernel,
        grid=(b, tps),
        in_specs=[grp(a) for a in os_] + [grp(a) for a in lses]
                 + [full((LANES, w)), full((w, d)), full((1, d)), row],
        out_specs=row,
        out_shape=jax.ShapeDtypeStruct((m, d), F32),
        scratch_shapes=[pltpu.VMEM((3, w // LANES, tm, LANES), F32), pltpu.VMEM((3, tm, LANES), F32)],
        compiler_params=_cparams(("arbitrary", "arbitrary")),
        name="combine_out_proj",
    )(*os_, *lses, jnp.asarray(expand), w_bf16, g.reshape(1, d), h2d)


def _out_kernel(a_ref, w_ref, g_ref, h_ref, out_ref):
    y = jnp.dot(a_ref[...], w_ref[...], preferred_element_type=F32)
    out_ref[...] = h_ref[...] + _rms(y, g_ref[...], RMS_EPS)


def _out_proj(a2d, w_bf16, g, h2d, tm=512):
    m, d = h2d.shape
    k = a2d.shape[1]
    return pl.pallas_call(
        _out_kernel,
        grid=(m // tm,),
        in_specs=[pl.BlockSpec((tm, k), lambda i: (i, 0)),
                  pl.BlockSpec((k, d), lambda i: (0, 0)),
                  pl.BlockSpec((1, d), lambda i: (0, 0)),
                  pl.BlockSpec((tm, d), lambda i: (i, 0))],
        out_specs=pl.BlockSpec((tm, d), lambda i: (i, 0)),
        out_shape=jax.ShapeDtypeStruct((m, d), F32),
        compiler_params=_cparams(("arbitrary",)),
        name="out_proj",
    )(a2d, w_bf16, g.reshape(1, d), h2d)


def _ffn_kernel(h_ref, g_in_ref, wup_ref, cw_ref, cb_ref, wdn_ref, g_out_ref, out_ref,
                halo_ref, acc_ref, *, tiles_per_seq, f_chunk):
    tm = h_ref.shape[0]
    d_ff = wdn_ref.shape[0]
    i = pl.program_id(0)
    h = h_ref[...]
    xn = _rms(h, g_in_ref[...], RMS_EPS).astype(BF16)
    row = lax.broadcasted_iota(jnp.int32, (tm, 1), 0)
    seq_start = (i % tiles_per_seq) == 0
    inv_sqrt2 = 1.0 / math.sqrt(2.0)

    def conv(u, cols):
        prev = jnp.where(seq_start, 0.0, halo_ref[:, cols])
        halo_ref[:, cols] = u[tm - 8:tm, :]
        u1 = pltpu.roll(u, 1, 0)
        u1 = jnp.where(row == 0, prev[7:8, :], u1)
        u2 = pltpu.roll(u, 2, 0)
        u2 = jnp.where(row == 0, prev[6:7, :], jnp.where(row == 1, prev[7:8, :], u2))
        return (cb_ref[:, cols] + cw_ref[0:1, cols] * u2 + cw_ref[1:2, cols] * u1 + cw_ref[2:3, cols] * u)

    for c in range(d_ff // f_chunk):
        gcols = slice(c * f_chunk, (c + 1) * f_chunk)
        vcols = slice(d_ff + c * f_chunk, d_ff + (c + 1) * f_chunk)
        gate = conv(jnp.dot(xn, wup_ref[:, gcols], preferred_element_type=F32), gcols)
        val = conv(jnp.dot(xn, wup_ref[:, vcols], preferred_element_type=F32), vcols)
        act = (0.5 * gate * (1.0 + lax.erf(gate * inv_sqrt2)) * val).astype(BF16)
        part = jnp.dot(act, wdn_ref[gcols, :], preferred_element_type=F32)
        if c == 0:
            acc_ref[...] = part
        else:
            acc_ref[...] += part
    out_ref[...] = h + _rms(acc_ref[...], g_out_ref[...], RMS_EPS)


def _conv_ffn(h2d, g_in, w_up_bf16, conv_w, conv_b, w_down_bf16, g_out, seq, tm=512, f_chunk=256):
    m, d = h2d.shape
    f2 = w_up_bf16.shape[1]
    d_ff = f2 // 2
    assert seq % tm == 0 and d_ff % f_chunk == 0
    full = lambda shp: pl.BlockSpec(shp, lambda i: (0, 0))
    return pl.pallas_call(
        functools.partial(_ffn_kernel, tiles_per_seq=seq // tm, f_chunk=f_chunk),
        grid=(m // tm,),
        in_specs=[pl.BlockSpec((tm, d), lambda i: (i, 0)),
                  full((1, d)), full((d, f2)), full((CONV_WIDTH, f2)), full((1, f2)),
                  full((d_ff, d)), full((1, d))],
        out_specs=pl.BlockSpec((tm, d), lambda i: (i, 0)),
        out_shape=jax.ShapeDtypeStruct((m, d), F32),
        scratch_shapes=[pltpu.VMEM((8, f2), F32), pltpu.VMEM((tm, d), F32)],
        compiler_params=_cparams(("arbitrary",)),
        name="conv_ffn",
    )(h2d, g_in.reshape(1, d), w_up_bf16, conv_w, conv_b.reshape(1, f2), w_down_bf16, g_out.reshape(1, d))


def _diff_kernel(q_ref, k_ref, v_ref, bias_ref, lq1, lk1, lq2, lk2, sg_ref, o_ref,
                 qs_sc, m_sc, l_sc, acc_sc, *, tile, n_bias, lambda_init, row_chunk):
    qi = pl.program_id(2)
    hd2 = 2 * HEAD_DIM
    lane = lax.broadcasted_iota(jnp.int32, (1, LANES), 1)
    q = q_ref[...]
    zero = jnp.zeros_like(q)
    scale = jnp.asarray(HEAD_DIM ** -0.5, q.dtype)
    qs_sc[0:tile, :] = jnp.where(lane < HEAD_DIM, q, zero) * scale
    qs_sc[tile:2 * tile, :] = jnp.where(lane >= HEAD_DIM, q, zero) * scale
    m_sc[...] = jnp.full(m_sc.shape, NEG_INF, F32)
    l_sc[...] = jnp.zeros(l_sc.shape, F32)
    acc_sc[...] = jnp.zeros(acc_sc.shape, F32)
    ones = jnp.ones((tile, LANES), BF16)

    def body(ki, carry):
        start = pl.multiple_of(ki * tile, tile)
        k = k_ref[pl.ds(start, tile), :]
        vx = jnp.concatenate([v_ref[pl.ds(start, tile), :], ones], axis=1)
        d = jnp.minimum(qi - ki, n_bias - 1)
        for c in range(2 * tile // row_chunk):
            rows = pl.ds(c * row_chunk, row_chunk)
            s = lax.dot_general(qs_sc[rows, :], k, (((1,), (1,)), ((), ())), preferred_element_type=F32)
            s = s + bias_ref[d, pl.ds((c * row_chunk) % tile, row_chunk), :]
            m_prev = m_sc[rows, :]
            m_new = jnp.maximum(m_prev, jnp.max(s, axis=-1, keepdims=True))
            alpha = jnp.exp(m_prev - m_new)
            p = jnp.exp(s - jnp.concatenate([m_new] * (tile // LANES), axis=1))
            pv = jnp.dot(p.astype(BF16), vx, preferred_element_type=F32)
            l_sc[rows, :] = alpha * l_sc[rows, :] + pv[:, hd2:]
            acc_sc[rows, :] = alpha * acc_sc[rows, :] + pv[:, :hd2]
            m_sc[rows, :] = m_new
        return carry

    lax.fori_loop(0, qi + 1, body, 0)

    lam = (jnp.exp(jnp.sum(lq1[...].astype(F32) * lk1[...].astype(F32), axis=-1, keepdims=True))
           - jnp.exp(jnp.sum(lq2[...].astype(F32) * lk2[...].astype(F32), axis=-1, keepdims=True))
           + lambda_init)
    o = acc_sc[...] * (1.0 / l_sc[...])
    a = o[0:tile, :] - lam * o[tile:2 * tile, :]
    y = _rms(a, sg_ref[...], SUBLN_EPS) * (1.0 - lambda_init)
    o_ref[...] = y.astype(o_ref.dtype)


def _diff_attention(q, kv, bias, lq1, lk1, lq2, lk2, subln_g, lambda_init, tile):
    b, s, _ = q.shape
    n_bias = bias.shape[1]
    hd2 = 2 * HEAD_DIM
    vec = lambda a: a.reshape(1, -1)
    small = lambda n: pl.BlockSpec((1, n), lambda h, bb, i: (0, 0))
    return pl.pallas_call(
        functools.partial(_diff_kernel, tile=tile, n_bias=n_bias, lambda_init=lambda_init,
                          row_chunk=min(DIFF_ROW_CHUNK, tile)),
        grid=(N_HEADS, b, s // tile),
        in_specs=[pl.BlockSpec((None, tile, hd2), lambda h, bb, i: (bb, i, h)),
                  pl.BlockSpec((None, s, hd2), lambda h, bb, i: (bb, 0, h)),
                  pl.BlockSpec((None, s, hd2), lambda h, bb, i: (bb, 0, N_HEADS + h)),
                  pl.BlockSpec((None, n_bias, tile, tile), lambda h, bb, i: (h, 0, 0, 0)),
                  small(HEAD_DIM), small(HEAD_DIM), small(HEAD_DIM), small(HEAD_DIM), small(hd2)],
        out_specs=pl.BlockSpec((None, tile, hd2), lambda h, bb, i: (bb, i, h)),
        out_shape=jax.ShapeDtypeStruct((b, s, N_HEADS * hd2), BF16),
        scratch_shapes=[pltpu.VMEM((2 * tile, hd2), BF16),
                        pltpu.VMEM((2 * tile, LANES), F32), pltpu.VMEM((2 * tile, LANES), F32),
                        pltpu.VMEM((2 * tile, hd2), F32)],
        compiler_params=_cparams(("arbitrary", "arbitrary", "arbitrary")),
        name="diff_attn",
    )(q, kv, kv, bias, vec(lq1), vec(lk1), vec(lq2), vec(lk2), vec(subln_g))


DIFF_TILE = 512
DIFF_ROW_CHUNK = 256


def kernel(x, rel_bias_table, norm_g, w_in_a, w_out_a, kv_norm_g, w_k_shared, w_v_shared, w_q_b,
           lam_q1, lam_k1, lam_q2, lam_k2, subln_g, w_out_b, w_up, conv_w, conv_b, w_down):
    b, s, d = x.shape
    depth = norm_g.shape[0]
    n_a = w_in_a.shape[0]
    bf = lambda a: a.astype(BF16)

    dil_bias = _bias_tiles(rel_bias_table, _dilated_bias_idx(), head_major=False)
    diff_tile = min(DIFF_TILE, s)
    diff_bias = _bias_tiles(rel_bias_table, _diff_bias_idx(s, diff_tile), head_major=True)

    h = x.reshape(b * s, d)
    kv = None
    for layer in range(depth):
        g = norm_g[layer]
        if layer < n_a:
            qkvs = _proj_a(h, g[0], bf(w_in_a[layer]), b, s)
            outs = [_dilated_group(qkvs[gi], dil_bias, gi, dil) for gi, (_, dil) in enumerate(DIL_GROUPS)]
            h = _combine_out([o for o, _ in outs], [l for _, l in outs], bf(w_out_a[layer]), g[1], h, b, s)
        else:
            j = layer - n_a
            lambda_init = 0.8 - 0.6 * math.exp(-0.3 * layer)
            q = _norm_matmul(h, g[0], bf(w_q_b[j])).reshape(b, s, -1)
            a = _diff_attention(q, kv, diff_bias, lam_q1[j], lam_k1[j], lam_q2[j], lam_k2[j],
                                subln_g[j], lambda_init, diff_tile)
            h = _out_proj(a.reshape(b * s, -1), bf(w_out_b[j]), g[1], h)
        h = _conv_ffn(h, g[2], bf(w_up[layer]), conv_w[layer], conv_b[layer], bf(w_down[layer]), g[3], s)
        if layer == n_a - 1:
            w_kv = jnp.concatenate([w_k_shared, w_v_shared], axis=1)
            kv = _norm_matmul(h, kv_norm_g, bf(w_kv)).reshape(b, s, -1)
    return h.reshape(b, s, d)
```

```python
import functools
import math

import numpy as np
import jax
import jax.numpy as jnp
from jax import lax
from jax.experimental import pallas as pl
from jax.experimental.pallas import tpu as pltpu

F32 = jnp.float32
BF16 = jnp.bfloat16

HEAD_DIM = 64
N_HEADS = 8
DIL_GROUPS = ((128, 1), (512, 4), (2048, 16))
ATTN_BLOCK = 128
NUM_BUCKETS = 32
MAX_DISTANCE = 2048
CONV_WIDTH = 3
RMS_EPS = 1e-6
SUBLN_EPS = 1e-5

LANES = 128
VMEM_LIMIT = 56 * 1024 * 1024

NEG_INF = float("-inf")


def _cparams(sem):
    return pltpu.CompilerParams(dimension_semantics=sem, vmem_limit_bytes=VMEM_LIMIT)


def _bucket_np(dist):
    n = np.maximum(dist, 0)
    max_exact = NUM_BUCKETS // 2

    def large(dtype):
        nf = np.maximum(n, 1).astype(dtype)
        v = np.log(nf / dtype(max_exact)) / dtype(math.log(MAX_DISTANCE / max_exact)) * dtype(NUM_BUCKETS - max_exact)
        return np.minimum(max_exact + v.astype(np.int32), NUM_BUCKETS - 1)

    l32, l64 = large(np.float32), large(np.float64)
    assert np.array_equal(l32, l64), "bucket boundary is rounding sensitive"
    return np.where(n < max_exact, n, l32).astype(np.int32)


def _bias_kernel(tbl_ref, idx_ref, o_ref):
    h = pl.program_id(1)
    idx = idx_ref[...]
    acc = jnp.full(idx.shape, NEG_INF, F32)
    for k in range(NUM_BUCKETS):
        acc = jnp.where(idx == k, tbl_ref[h, k], acc)
    o_ref[...] = acc


def _bias_tiles(table, idx, head_major):
    t, r, c = idx.shape
    nh = table.shape[0]
    if head_major:
        out_shape = (nh, t, r, c)
        out_map = lambda i, h: (h, i, 0, 0)
    else:
        out_shape = (t, nh, r, c)
        out_map = lambda i, h: (i, h, 0, 0)
    return pl.pallas_call(
        _bias_kernel,
        grid=(t, nh),
        in_specs=[pl.BlockSpec(memory_space=pltpu.SMEM),
                  pl.BlockSpec((None, r, c), lambda i, h: (i, 0, 0))],
        out_specs=pl.BlockSpec((None, None, r, c), out_map),
        out_shape=jax.ShapeDtypeStruct(out_shape, F32),
        compiler_params=_cparams(("arbitrary", "arbitrary")),
        name="rel_bias_tiles",
    )(table, jnp.asarray(idx))


def _dilated_bias_idx():
    qi = np.arange(ATTN_BLOCK)[:, None]
    ki = np.arange(2 * ATTN_BLOCK)[None, :]
    dist_u = qi + ATTN_BLOCK - ki
    out = []
    for window, dil in DIL_GROUPS:
        band = (dist_u >= 0) & (dist_u <= window // dil)
        out.append(np.where(band, _bucket_np(dist_u * dil), -1))
    return np.stack(out).astype(np.int32)


def _diff_bias_idx(seq, tile):
    nq = seq // tile
    i = np.arange(tile)[:, None]
    j = np.arange(tile)[None, :]
    tiles = []
    for d in range(nq):
        dist = d * tile + i - j
        tiles.append(np.where(dist >= 0, _bucket_np(dist), -1).astype(np.int32))
    nd = nq
    while nd > 1 and np.array_equal(tiles[nd - 1], tiles[nd - 2]):
        nd -= 1
    assert all(np.array_equal(tiles[d], tiles[nd - 1]) for d in range(nd - 1, nq))
    return np.stack(tiles[:nd])


def _rms(x, g, eps):
    return x * lax.rsqrt(jnp.mean(x * x, axis=-1, keepdims=True) + eps) * g


def _norm_matmul_kernel(x_ref, g_ref, w_ref, o_ref, *, n_chunk):
    xn = _rms(x_ref[...], g_ref[...], RMS_EPS).astype(BF16)
    n = o_ref.shape[-1]
    for j in range(n // n_chunk):
        sl = slice(j * n_chunk, (j + 1) * n_chunk)
        o_ref[:, sl] = jnp.dot(xn, w_ref[:, sl], preferred_element_type=F32).astype(o_ref.dtype)


def _norm_matmul(x2d, g, w_bf16, tm=512, n_chunk=512):
    m, d = x2d.shape
    n = w_bf16.shape[1]
    return pl.pallas_call(
        functools.partial(_norm_matmul_kernel, n_chunk=n_chunk),
        grid=(m // tm,),
        in_specs=[pl.BlockSpec((tm, d), lambda i: (i, 0)),
                  pl.BlockSpec((1, d), lambda i: (0, 0)),
                  pl.BlockSpec((d, n), lambda i: (0, 0))],
        out_specs=pl.BlockSpec((tm, n), lambda i: (i, 0)),
        out_shape=jax.ShapeDtypeStruct((m, n), BF16),
        compiler_params=_cparams(("arbitrary",)),
        name="norm_matmul",
    )(x2d, g.reshape(1, d), w_bf16)


def _proj_a_kernel(x_ref, g_ref, w_ref, *refs, n_chunk):
    out_refs, xn_sc = refs[:-1], refs[-1]
    tm = x_ref.shape[0]
    xn = _rms(x_ref[...], g_ref[...], RMS_EPS)
    nlb = xn.shape[1] // LANES
    for j in range(nlb):
        xn_sc[j] = xn[:, j * LANES:(j + 1) * LANES]
    gw = out_refs[0].shape[-1]
    for gi, (o_ref, (_, dil)) in enumerate(zip(out_refs, DIL_GROUPS)):
        per = tm // dil
        if dil == 1:
            xg = xn
        else:
            xg = jnp.concatenate(
                [jnp.concatenate([xn_sc[j, pl.ds(c, per, stride=dil), :] for c in range(dil)], axis=0)
                 for j in range(nlb)], axis=1)
        xg = xg.astype(BF16)
        for j in range(gw // n_chunk):
            res = jnp.dot(xg, w_ref[:, gi * gw + j * n_chunk:gi * gw + (j + 1) * n_chunk],
                          preferred_element_type=F32).astype(o_ref.dtype)
            for c in range(dil):
                o_ref[c, :, j * n_chunk:(j + 1) * n_chunk] = res[c * per:(c + 1) * per, :]


def _proj_a(x2d, g, w_bf16, b, s, tm=512, n_chunk=512):
    m, d = x2d.shape
    gw = w_bf16.shape[1] // len(DIL_GROUPS)
    tps = s // tm
    assert all(tm % (16 * dil) == 0 for _, dil in DIL_GROUPS)
    return pl.pallas_call(
        functools.partial(_proj_a_kernel, n_chunk=n_chunk),
        grid=(b, tps),
        in_specs=[pl.BlockSpec((tm, d), lambda bb, i: (bb * tps + i, 0)),
                  pl.BlockSpec((1, d), lambda bb, i: (0, 0)),
                  pl.BlockSpec(w_bf16.shape, lambda bb, i: (0, 0))],
        out_specs=[pl.BlockSpec((None, dil, tm // dil, gw), lambda bb, i: (bb, 0, i, 0)) for _, dil in DIL_GROUPS],
        out_shape=[jax.ShapeDtypeStruct((b, dil, s // dil, gw), BF16) for _, dil in DIL_GROUPS],
        scratch_shapes=[pltpu.VMEM((d // LANES, tm, LANES), F32)],
        compiler_params=_cparams(("arbitrary", "arbitrary")),
        name="proj_dilated",
    )(x2d, g.reshape(1, d), w_bf16)


def _dil_kernel(q_ref, k_ref, v_ref, kp_ref, vp_ref, bias_ref, o_ref, lse_ref, *, tq):
    blk = ATTN_BLOCK
    i = pl.program_id(2)
    lane = lax.broadcasted_iota(jnp.int32, (1, LANES), 1)
    col = lax.broadcasted_iota(jnp.int32, (blk, 2 * blk), 1)
    first_pen = jnp.where(jnp.logical_and(i == 0, col < blk), NEG_INF, 0.0).astype(F32)
    scale = HEAD_DIM ** -0.5
    for j in range(tq // blk):
        rows = slice(j * blk, (j + 1) * blk)
        qj = q_ref[rows, :]
        if j == 0:
            kx = jnp.concatenate([kp_ref[...], k_ref[0:blk, :]], axis=0)
            vx = jnp.concatenate([vp_ref[...], v_ref[0:blk, :]], axis=0)
        else:
            kx = k_ref[(j - 1) * blk:(j + 1) * blk, :]
            vx = v_ref[(j - 1) * blk:(j + 1) * blk, :]
        lse_tile = jnp.zeros((blk, LANES), F32)
        for hp in range(N_HEADS // 2):
            cols = slice(hp * LANES, (hp + 1) * LANES)
            q2, k2, v2 = qj[:, cols], kx[:, cols], vx[:, cols]
            o_pair = None
            for e in range(2):
                h = 2 * hp + e
                sel = (lane >= HEAD_DIM) if e else (lane < HEAD_DIM)
                qm = jnp.where(sel, q2, jnp.zeros_like(q2))
                s = lax.dot_general(qm, k2, (((1,), (1,)), ((), ())), preferred_element_type=F32)
                s = s * scale + bias_ref[h]
                if j == 0:
                    s = s + first_pen
                m = jnp.max(s, axis=-1, keepdims=True)
                p = jnp.exp(s - m)
                l = jnp.sum(p, axis=-1, keepdims=True)
                pv = jnp.dot(p.astype(BF16), v2, preferred_element_type=F32)
                o_e = pv * (1.0 / l)
                o_pair = o_e if e == 0 else jnp.where(lane < HEAD_DIM, o_pair, o_e)
                lse_tile = jnp.where(lane == h, m + jnp.log(l), lse_tile)
            o_ref[rows, cols] = o_pair
        lse_ref[rows, :] = lse_tile


def _dilated_group(qkv, bias, g, dil):
    b, r, n, _ = qkv.shape
    w = N_HEADS * HEAD_DIM
    assert r == dil and n % ATTN_BLOCK == 0
    tq = min(n, 2 * ATTN_BLOCK)
    sub = tq // ATTN_BLOCK

    def cur(which):
        return pl.BlockSpec((None, None, tq, w), lambda bb, c, i: (bb, c, i, which))

    def prev(which):
        return pl.BlockSpec((None, None, ATTN_BLOCK, w),
                            lambda bb, c, i: (bb, c, jnp.maximum(i * sub - 1, 0), which))

    return pl.pallas_call(
        functools.partial(_dil_kernel, tq=tq),
        grid=(b, dil, n // tq),
        in_specs=[cur(0), cur(1), cur(2), prev(1), prev(2),
                  pl.BlockSpec((None, N_HEADS, ATTN_BLOCK, 2 * ATTN_BLOCK), lambda bb, c, i: (g, 0, 0, 0))],
        out_specs=[pl.BlockSpec((None, None, tq, w), lambda bb, c, i: (bb, c, i, 0)),
                   pl.BlockSpec((None, None, tq, LANES), lambda bb, c, i: (bb, c, i, 0))],
        out_shape=[jax.ShapeDtypeStruct((b, dil, n, w), F32),
                   jax.ShapeDtypeStruct((b, dil, n, LANES), F32)],
        compiler_params=_cparams(("arbitrary", "arbitrary", "arbitrary")),
        name=f"dilated_attn_g{g}",
    )(qkv, qkv, qkv, qkv, qkv, bias)


def _combine_out_kernel(o0, o1, o2, l0, l1, l2, e_ref, w_ref, g_ref, h_ref, out_ref, o_sc, l_sc):
    tm = h_ref.shape[0]

    def token_order(o_ref, l_ref, slot):
        dil = o_ref.shape[0]
        if dil == 1:
            return o_ref[0], l_ref[0]
        per = tm // dil
        nlb = o_ref.shape[-1] // LANES
        for c in range(dil):
            oc = o_ref[c]
            for j in range(nlb):
                o_sc[slot, j, pl.ds(c, per, stride=dil), :] = oc[:, j * LANES:(j + 1) * LANES]
            l_sc[slot, pl.ds(c, per, stride=dil), :] = l_ref[c]
        return jnp.concatenate([o_sc[slot, j] for j in range(nlb)], axis=1), l_sc[slot]

    pairs = [token_order(o_ref, l_ref, slot) for slot, (o_ref, l_ref) in enumerate(((o0, l0), (o1, l1), (o2, l2)))]
    ls = [l for _, l in pairs]
    mx = jnp.maximum(jnp.maximum(ls[0], ls[1]), ls[2])
    ws = [jnp.exp(l - mx) for l in ls]
    inv = 1.0 / (ws[0] + ws[1] + ws[2])
    a = None
    for (og, _), wg in zip(pairs, ws):
        ax = jnp.dot(wg * inv, e_ref[...], preferred_element_type=F32, precision=lax.Precision.HIGHEST)
        t = ax * og
        a = t if a is None else a + t
    y = jnp.dot(a.astype(BF16), w_ref[...], preferred_element_type=F32)
    out_ref[...] = h_ref[...] + _rms(y, g_ref[...], RMS_EPS)


def _combine_out(os_, lses, w_bf16, g, h2d, b, s, tm=512):
    m, d = h2d.shape
    w = os_[0].shape[-1]
    tps = s // tm
    expand = np.zeros((LANES, w), np.float32)
    for hh in range(N_HEADS):
        expand[hh, hh * HEAD_DIM:(hh + 1) * HEAD_DIM] = 1.0
    grp = lambda a: pl.BlockSpec((None, a.shape[1], tm // a.shape[1], a.shape[3]), lambda bb, i: (bb, 0, i, 0))
    row = pl.BlockSpec((tm, d), lambda bb, i: (bb * tps + i, 0))
    full = lambda shp: pl.BlockSpec(shp, lambda bb, i: (0, 0))
    return pl.pallas_call(
        _combine_out_kernel,
        grid=(b, tps),
        in_specs=[grp(a) for a in os_] + [grp(a) for a in lses]
                 + [full((LANES, w)), full((w, d)), full((1, d)), row],
        out_specs=row,
        out_shape=jax.ShapeDtypeStruct((m, d), F32),
        scratch_shapes=[pltpu.VMEM((3, w // LANES, tm, LANES), F32), pltpu.VMEM((3, tm, LANES), F32)],
        compiler_params=_cparams(("arbitrary", "arbitrary")),
        name="combine_out_proj",
    )(*os_, *lses, jnp.asarray(expand), w_bf16, g.reshape(1, d), h2d)


def _out_kernel(a_ref, w_ref, g_ref, h_ref, out_ref):
    y = jnp.dot(a_ref[...], w_ref[...], preferred_element_type=F32)
    out_ref[...] = h_ref[...] + _rms(y, g_ref[...], RMS_EPS)


def _out_proj(a2d, w_bf16, g, h2d, tm=512):
    m, d = h2d.shape
    k = a2d.shape[1]
    return pl.pallas_call(
        _out_kernel,
        grid=(m // tm,),
        in_specs=[pl.BlockSpec((tm, k), lambda i: (i, 0)),
                  pl.BlockSpec((k, d), lambda i: (0, 0)),
                  pl.BlockSpec((1, d), lambda i: (0, 0)),
                  pl.BlockSpec((tm, d), lambda i: (i, 0))],
        out_specs=pl.BlockSpec((tm, d), lambda i: (i, 0)),
        out_shape=jax.ShapeDtypeStruct((m, d), F32),
        compiler_params=_cparams(("arbitrary",)),
        name="out_proj",
    )(a2d, w_bf16, g.reshape(1, d), h2d)


def _ffn_kernel(h_ref, g_in_ref, wup_ref, cw_ref, cb_ref, wdn_ref, g_out_ref, out_ref,
                xn_sc, ua_sc, ub_sc, halo_sc, acc_sc, *, tiles_per_seq):
    tm = h_ref.shape[0]
    nc = wdn_ref.shape[0]
    i = pl.program_id(0)
    xn_sc[...] = _rms(h_ref[...], g_in_ref[...], RMS_EPS).astype(BF16)
    acc_sc[...] = jnp.zeros(acc_sc.shape, F32)
    seq_start = (i % tiles_per_seq) == 0
    inv_sqrt2 = 1.0 / math.sqrt(2.0)

    def up(c, u_sc):
        u_sc[0, 8:tm + 8, :] = jnp.dot(xn_sc[...], wup_ref[c], preferred_element_type=F32)
        u_sc[1, 8:tm + 8, :] = jnp.dot(xn_sc[...], wup_ref[nc + c], preferred_element_type=F32)

    def conv(u_sc, j, cc):
        cw = cw_ref[cc]
        return (cb_ref[cc] + cw[0:1, :] * u_sc[j, 6:tm + 6, :] + cw[1:2, :] * u_sc[j, 7:tm + 7, :]
                + cw[2:3, :] * u_sc[j, 8:tm + 8, :])

    def down(c, u_sc):
        u_sc[:, 0:8, :] = jnp.where(seq_start, 0.0, halo_sc[c])
        halo_sc[c] = u_sc[:, tm:tm + 8, :]
        gate = conv(u_sc, 0, c)
        val = conv(u_sc, 1, nc + c)
        act = (0.5 * gate * (1.0 + lax.erf(gate * inv_sqrt2)) * val).astype(BF16)
        acc_sc[...] += jnp.dot(act, wdn_ref[c], preferred_element_type=F32)

    up(0, ua_sc)

    def body(k, carry):
        c = 2 * k
        up(c + 1, ub_sc)
        down(c, ua_sc)
        up(c + 2, ua_sc)
        down(c + 1, ub_sc)
        return carry

    assert nc % 2 == 1
    lax.fori_loop(0, nc // 2, body, 0)
    down(nc - 1, ua_sc)
    out_ref[...] = h_ref[...] + _rms(acc_sc[...], g_out_ref[...], RMS_EPS)


def _conv_ffn(h2d, g_in, w_up, conv_w, conv_b, w_down, g_out, seq, tm=512, f_chunk=256):
    m, d = h2d.shape
    f2 = w_up.shape[1]
    d_ff = f2 // 2
    assert seq % tm == 0 and d_ff % f_chunk == 0
    nc = d_ff // f_chunk
    wup_c = w_up.astype(BF16).reshape(d, 2 * nc, f_chunk).transpose(1, 0, 2)
    wdn_c = w_down.astype(BF16).reshape(nc, f_chunk, d)
    cw_c = conv_w.reshape(CONV_WIDTH, 2 * nc, f_chunk).transpose(1, 0, 2)
    cb_c = conv_b.reshape(2 * nc, 1, f_chunk)
    full = lambda a: pl.BlockSpec(a.shape, lambda i: (0,) * a.ndim)
    g_in2, g_out2 = g_in.reshape(1, d), g_out.reshape(1, d)
    return pl.pallas_call(
        functools.partial(_ffn_kernel, tiles_per_seq=seq // tm),
        grid=(m // tm,),
        in_specs=[pl.BlockSpec((tm, d), lambda i: (i, 0)),
                  full(g_in2), full(wup_c), full(cw_c), full(cb_c), full(wdn_c), full(g_out2)],
        out_specs=pl.BlockSpec((tm, d), lambda i: (i, 0)),
        out_shape=jax.ShapeDtypeStruct((m, d), F32),
        scratch_shapes=[pltpu.VMEM((tm, d), BF16),
                        pltpu.VMEM((2, tm + 8, f_chunk), F32), pltpu.VMEM((2, tm + 8, f_chunk), F32),
                        pltpu.VMEM((nc, 2, 8, f_chunk), F32), pltpu.VMEM((tm, d), F32)],
        compiler_params=_cparams(("arbitrary",)),
        name="conv_ffn",
    )(h2d, g_in2, wup_c, cw_c, cb_c, wdn_c, g_out2)


def _diff_kernel(q_ref, k_ref, v_ref, bias_ref, lq1, lk1, lq2, lk2, sg_ref, o_ref,
                 qs_sc, s_sc, m_sc, l_sc, acc_sc, *, tile, n_bias, lambda_init, row_chunk):
    qi = pl.program_id(2)
    hd2 = 2 * HEAD_DIM
    lane = lax.broadcasted_iota(jnp.int32, (1, LANES), 1)
    q = q_ref[...]
    zero = jnp.zeros_like(q)
    scale = jnp.asarray(HEAD_DIM ** -0.5, q.dtype)
    qs_sc[0:tile, :] = jnp.where(lane < HEAD_DIM, q, zero) * scale
    qs_sc[tile:2 * tile, :] = jnp.where(lane >= HEAD_DIM, q, zero) * scale
    m_sc[...] = jnp.full(m_sc.shape, NEG_INF, F32)
    l_sc[...] = jnp.zeros(l_sc.shape, F32)
    acc_sc[...] = jnp.zeros(acc_sc.shape, F32)
    ones = jnp.ones((tile, LANES), BF16)
    n_chunks = 2 * tile // row_chunk

    def scores(ki, c):
        k = k_ref[pl.ds(pl.multiple_of(ki * tile, tile), tile), :]
        return lax.dot_general(qs_sc[pl.ds(c * row_chunk, row_chunk), :], k, (((1,), (1,)), ((), ())),
                               preferred_element_type=F32)

    def step(ki, prefetch):
        vx = jnp.concatenate([v_ref[pl.ds(pl.multiple_of(ki * tile, tile), tile), :], ones], axis=1)
        d = jnp.minimum(qi - ki, n_bias - 1)
        for c in range(n_chunks):
            rows = pl.ds(c * row_chunk, row_chunk)
            s = s_sc[rows, :] + bias_ref[d, pl.ds((c * row_chunk) % tile, row_chunk), :]
            if prefetch:
                s_sc[rows, :] = scores(ki + 1, c)
            m_prev = m_sc[rows, :]
            m_new = jnp.maximum(m_prev, jnp.max(s, axis=-1, keepdims=True))
            alpha = jnp.exp(m_prev - m_new)
            p = jnp.exp(s - jnp.concatenate([m_new] * (tile // LANES), axis=1))
            pv = jnp.dot(p.astype(BF16), vx, preferred_element_type=F32)
            l_sc[rows, :] = alpha * l_sc[rows, :] + pv[:, hd2:]
            acc_sc[rows, :] = alpha * acc_sc[rows, :] + pv[:, :hd2]
            m_sc[rows, :] = m_new

    for c in range(n_chunks):
        s_sc[pl.ds(c * row_chunk, row_chunk), :] = scores(0, c)

    def body(ki, carry):
        step(ki, True)
        return carry

    lax.fori_loop(0, qi, body, 0)
    step(qi, False)

    lam = (jnp.exp(jnp.sum(lq1[...].astype(F32) * lk1[...].astype(F32), axis=-1, keepdims=True))
           - jnp.exp(jnp.sum(lq2[...].astype(F32) * lk2[...].astype(F32), axis=-1, keepdims=True))
           + lambda_init)
    o = acc_sc[...] * (1.0 / l_sc[...])
    a = o[0:tile, :] - lam * o[tile:2 * tile, :]
    y = _rms(a, sg_ref[...], SUBLN_EPS) * (1.0 - lambda_init)
    o_ref[...] = y.astype(o_ref.dtype)


def _diff_attention(q, kv, bias, lq1, lk1, lq2, lk2, subln_g, lambda_init, tile):
    b, s, _ = q.shape
    n_bias = bias.shape[1]
    hd2 = 2 * HEAD_DIM
    vec = lambda a: a.reshape(1, -1)
    small = lambda n: pl.BlockSpec((1, n), lambda h, bb, i: (0, 0))
    return pl.pallas_call(
        functools.partial(_diff_kernel, tile=tile, n_bias=n_bias, lambda_init=lambda_init,
                          row_chunk=min(DIFF_ROW_CHUNK, tile)),
        grid=(N_HEADS, b, s // tile),
        in_specs=[pl.BlockSpec((None, tile, hd2), lambda h, bb, i: (bb, i, h)),
                  pl.BlockSpec((None, s, hd2), lambda h, bb, i: (bb, 0, h)),
                  pl.BlockSpec((None, s, hd2), lambda h, bb, i: (bb, 0, N_HEADS + h)),
                  pl.BlockSpec((None, n_bias, tile, tile), lambda h, bb, i: (h, 0, 0, 0)),
                  small(HEAD_DIM), small(HEAD_DIM), small(HEAD_DIM), small(HEAD_DIM), small(hd2)],
        out_specs=pl.BlockSpec((None, tile, hd2), lambda h, bb, i: (bb, i, h)),
        out_shape=jax.ShapeDtypeStruct((b, s, N_HEADS * hd2), BF16),
        scratch_shapes=[pltpu.VMEM((2 * tile, hd2), BF16), pltpu.VMEM((2 * tile, tile), F32),
                        pltpu.VMEM((2 * tile, LANES), F32), pltpu.VMEM((2 * tile, LANES), F32),
                        pltpu.VMEM((2 * tile, hd2), F32)],
        compiler_params=_cparams(("arbitrary", "arbitrary", "arbitrary")),
        name="diff_attn",
    )(q, kv, kv, bias, vec(lq1), vec(lk1), vec(lq2), vec(lk2), vec(subln_g))


DIFF_TILE = 512
DIFF_ROW_CHUNK = 512


def kernel(x, rel_bias_table, norm_g, w_in_a, w_out_a, kv_norm_g, w_k_shared, w_v_shared, w_q_b,
           lam_q1, lam_k1, lam_q2, lam_k2, subln_g, w_out_b, w_up, conv_w, conv_b, w_down):
    b, s, d = x.shape
    depth = norm_g.shape[0]
    n_a = w_in_a.shape[0]
    bf = lambda a: a.astype(BF16)

    dil_bias = _bias_tiles(rel_bias_table, _dilated_bias_idx(), head_major=False)
    diff_tile = min(DIFF_TILE, s)
    diff_bias = _bias_tiles(rel_bias_table, _diff_bias_idx(s, diff_tile), head_major=True)

    h = x.reshape(b * s, d)
    kv = None
    for layer in range(depth):
        g = norm_g[layer]
        if layer < n_a:
            qkvs = _proj_a(h, g[0], bf(w_in_a[layer]), b, s)
            outs = [_dilated_group(qkvs[gi], dil_bias, gi, dil) for gi, (_, dil) in enumerate(DIL_GROUPS)]
            h = _combine_out([o for o, _ in outs], [l for _, l in outs], bf(w_out_a[layer]), g[1], h, b, s)
        else:
            j = layer - n_a
            lambda_init = 0.8 - 0.6 * math.exp(-0.3 * layer)
            q = _norm_matmul(h, g[0], bf(w_q_b[j])).reshape(b, s, -1)
            a = _diff_attention(q, kv, diff_bias, lam_q1[j], lam_k1[j], lam_q2[j], lam_k2[j],
                                subln_g[j], lambda_init, diff_tile)
            h = _out_proj(a.reshape(b * s, -1), bf(w_out_b[j]), g[1], h)
        h = _conv_ffn(h, g[2], w_up[layer], conv_w[layer], conv_b[layer], w_down[layer], g[3], s)
        if layer == n_a - 1:
            w_kv = jnp.concatenate([w_k_shared, w_v_shared], axis=1)
            kv = _norm_matmul(h, kv_norm_g, bf(w_kv)).reshape(b, s, -1)
    return h.reshape(b, s, d)
```

```python
import functools
import math

import numpy as np
import jax
import jax.numpy as jnp
from jax import lax
from jax.experimental import pallas as pl
from jax.experimental.pallas import tpu as pltpu

F32 = jnp.float32
BF16 = jnp.bfloat16

HEAD_DIM = 64
N_HEADS = 8
DIL_GROUPS = ((128, 1), (512, 4), (2048, 16))
ATTN_BLOCK = 128
NUM_BUCKETS = 32
MAX_DISTANCE = 2048
CONV_WIDTH = 3
RMS_EPS = 1e-6
SUBLN_EPS = 1e-5

LANES = 128
VMEM_LIMIT = 56 * 1024 * 1024

NEG_INF = float("-inf")


def _cparams(sem):
    return pltpu.CompilerParams(dimension_semantics=sem, vmem_limit_bytes=VMEM_LIMIT)


def _bucket_np(dist):
    n = np.maximum(dist, 0)
    max_exact = NUM_BUCKETS // 2

    def large(dtype):
        nf = np.maximum(n, 1).astype(dtype)
        v = np.log(nf / dtype(max_exact)) / dtype(math.log(MAX_DISTANCE / max_exact)) * dtype(NUM_BUCKETS - max_exact)
        return np.minimum(max_exact + v.astype(np.int32), NUM_BUCKETS - 1)

    l32, l64 = large(np.float32), large(np.float64)
    assert np.array_equal(l32, l64), "bucket boundary is rounding sensitive"
    return np.where(n < max_exact, n, l32).astype(np.int32)


def _bias_kernel(lo_ref, hi_ref, tbl_ref, idx_ref, o_ref):
    t = pl.program_id(0)
    h = pl.program_id(1)
    idx = idx_ref[...]

    def body(k, acc):
        return jnp.where(idx == k, tbl_ref[h, k], acc)

    o_ref[...] = lax.fori_loop(lo_ref[t], hi_ref[t] + 1, body, jnp.full(idx.shape, NEG_INF, F32))


def _bias_tiles(table, idx, head_major):
    t, r, c = idx.shape
    nh = table.shape[0]
    lo = np.array([idx[i][idx[i] >= 0].min() for i in range(t)], np.int32)
    hi = np.array([idx[i][idx[i] >= 0].max() for i in range(t)], np.int32)
    if head_major:
        out_shape = (nh, t, r, c)
        out_map = lambda i, h: (h, i, 0, 0)
    else:
        out_shape = (t, nh, r, c)
        out_map = lambda i, h: (i, h, 0, 0)
    smem = pl.BlockSpec(memory_space=pltpu.SMEM)
    return pl.pallas_call(
        _bias_kernel,
        grid=(t, nh),
        in_specs=[smem, smem, smem, pl.BlockSpec((None, r, c), lambda i, h: (i, 0, 0))],
        out_specs=pl.BlockSpec((None, None, r, c), out_map),
        out_shape=jax.ShapeDtypeStruct(out_shape, F32),
        compiler_params=_cparams(("arbitrary", "arbitrary")),
        name="rel_bias_tiles",
    )(jnp.asarray(lo), jnp.asarray(hi), table, jnp.asarray(idx))


def _dilated_bias_idx():
    qi = np.arange(ATTN_BLOCK)[:, None]
    ki = np.arange(2 * ATTN_BLOCK)[None, :]
    dist_u = qi + ATTN_BLOCK - ki
    out = []
    for window, dil in DIL_GROUPS:
        band = (dist_u >= 0) & (dist_u <= window // dil)
        out.append(np.where(band, _bucket_np(dist_u * dil), -1))
    return np.stack(out).astype(np.int32)


def _diff_bias_idx(seq, tile):
    nq = seq // tile
    i = np.arange(tile)[:, None]
    j = np.arange(tile)[None, :]
    tiles = []
    for d in range(nq):
        dist = d * tile + i - j
        tiles.append(np.where(dist >= 0, _bucket_np(dist), -1).astype(np.int32))
    nd = nq
    while nd > 1 and np.array_equal(tiles[nd - 1], tiles[nd - 2]):
        nd -= 1
    assert all(np.array_equal(tiles[d], tiles[nd - 1]) for d in range(nd - 1, nq))
    return np.stack(tiles[:nd])


def _rms(x, g, eps):
    return x * lax.rsqrt(jnp.mean(x * x, axis=-1, keepdims=True) + eps) * g


def _norm_matmul_kernel(x_ref, g_ref, w_ref, o_ref, *, n_chunk):
    xn = _rms(x_ref[...], g_ref[...], RMS_EPS).astype(BF16)
    n = o_ref.shape[-1]
    for j in range(n // n_chunk):
        sl = slice(j * n_chunk, (j + 1) * n_chunk)
        o_ref[:, sl] = jnp.dot(xn, w_ref[:, sl], preferred_element_type=F32).astype(o_ref.dtype)


def _norm_matmul(x2d, g, w_bf16, tm=512, n_chunk=512):
    m, d = x2d.shape
    n = w_bf16.shape[1]
    return pl.pallas_call(
        functools.partial(_norm_matmul_kernel, n_chunk=n_chunk),
        grid=(m // tm,),
        in_specs=[pl.BlockSpec((tm, d), lambda i: (i, 0)),
                  pl.BlockSpec((1, d), lambda i: (0, 0)),
                  pl.BlockSpec((d, n), lambda i: (0, 0))],
        out_specs=pl.BlockSpec((tm, n), lambda i: (i, 0)),
        out_shape=jax.ShapeDtypeStruct((m, n), BF16),
        compiler_params=_cparams(("arbitrary",)),
        name="norm_matmul",
    )(x2d, g.reshape(1, d), w_bf16)


def _proj_a_kernel(x_ref, g_ref, w_ref, *refs, n_chunk):
    out_refs, xn_sc = refs[:-1], refs[-1]
    tm = x_ref.shape[0]
    xn = _rms(x_ref[...], g_ref[...], RMS_EPS)
    nlb = xn.shape[1] // LANES
    for j in range(nlb):
        xn_sc[j] = xn[:, j * LANES:(j + 1) * LANES]
    gw = out_refs[0].shape[-1]
    for gi, (o_ref, (_, dil)) in enumerate(zip(out_refs, DIL_GROUPS)):
        per = tm // dil
        if dil == 1:
            xg = xn
        else:
            xg = jnp.concatenate(
                [jnp.concatenate([xn_sc[j, pl.ds(c, per, stride=dil), :] for c in range(dil)], axis=0)
                 for j in range(nlb)], axis=1)
        xg = xg.astype(BF16)
        for j in range(gw // n_chunk):
            res = jnp.dot(xg, w_ref[:, gi * gw + j * n_chunk:gi * gw + (j + 1) * n_chunk],
                          preferred_element_type=F32).astype(o_ref.dtype)
            for c in range(dil):
                o_ref[c, :, j * n_chunk:(j + 1) * n_chunk] = res[c * per:(c + 1) * per, :]


def _proj_a(x2d, g, w_bf16, b, s, tm=512, n_chunk=512):
    m, d = x2d.shape
    gw = w_bf16.shape[1] // len(DIL_GROUPS)
    tps = s // tm
    assert all(tm % (16 * dil) == 0 for _, dil in DIL_GROUPS)
    return pl.pallas_call(
        functools.partial(_proj_a_kernel, n_chunk=n_chunk),
        grid=(b, tps),
        in_specs=[pl.BlockSpec((tm, d), lambda bb, i: (bb * tps + i, 0)),
                  pl.BlockSpec((1, d), lambda bb, i: (0, 0)),
                  pl.BlockSpec(w_bf16.shape, lambda bb, i: (0, 0))],
        out_specs=[pl.BlockSpec((None, dil, tm // dil, gw), lambda bb, i: (bb, 0, i, 0)) for _, dil in DIL_GROUPS],
        out_shape=[jax.ShapeDtypeStruct((b, dil, s // dil, gw), BF16) for _, dil in DIL_GROUPS],
        scratch_shapes=[pltpu.VMEM((d // LANES, tm, LANES), F32)],
        compiler_params=_cparams(("arbitrary", "arbitrary")),
        name="proj_dilated",
    )(x2d, g.reshape(1, d), w_bf16)


def _dil_kernel(q_ref, k_ref, v_ref, kp_ref, vp_ref, bias_ref, o_ref, lse_ref, *, tq):
    blk = ATTN_BLOCK
    i = pl.program_id(2)
    lane = lax.broadcasted_iota(jnp.int32, (1, LANES), 1)
    col = lax.broadcasted_iota(jnp.int32, (blk, 2 * blk), 1)
    first_pen = jnp.where(jnp.logical_and(i == 0, col < blk), NEG_INF, 0.0).astype(F32)
    scale = HEAD_DIM ** -0.5
    for j in range(tq // blk):
        rows = slice(j * blk, (j + 1) * blk)
        qj = q_ref[rows, :]
        if j == 0:
            kx = jnp.concatenate([kp_ref[...], k_ref[0:blk, :]], axis=0)
            vx = jnp.concatenate([vp_ref[...], v_ref[0:blk, :]], axis=0)
        else:
            kx = k_ref[(j - 1) * blk:(j + 1) * blk, :]
            vx = v_ref[(j - 1) * blk:(j + 1) * blk, :]
        lse_tile = jnp.zeros((blk, LANES), F32)
        for hp in range(N_HEADS // 2):
            cols = slice(hp * LANES, (hp + 1) * LANES)
            q2, k2, v2 = qj[:, cols], kx[:, cols], vx[:, cols]
            o_pair = None
            for e in range(2):
                h = 2 * hp + e
                sel = (lane >= HEAD_DIM) if e else (lane < HEAD_DIM)
                qm = jnp.where(sel, q2, jnp.zeros_like(q2))
                s = lax.dot_general(qm, k2, (((1,), (1,)), ((), ())), preferred_element_type=F32)
                s = s * scale + bias_ref[h]
                if j == 0:
                    s = s + first_pen
                m = jnp.max(s, axis=-1, keepdims=True)
                p = jnp.exp(s - m)
                l = jnp.sum(p, axis=-1, keepdims=True)
                pv = jnp.dot(p.astype(BF16), v2, preferred_element_type=F32)
                o_e = pv * (1.0 / l)
                o_pair = o_e if e == 0 else jnp.where(lane < HEAD_DIM, o_pair, o_e)
                lse_tile = jnp.where(lane == h, m + jnp.log(l), lse_tile)
            o_ref[rows, cols] = o_pair.astype(o_ref.dtype)
        lse_ref[rows, :] = lse_tile


def _dilated_group(qkv, bias, g, dil):
    b, r, n, _ = qkv.shape
    w = N_HEADS * HEAD_DIM
    assert r == dil and n % ATTN_BLOCK == 0
    tq = min(n, 2 * ATTN_BLOCK)
    sub = tq // ATTN_BLOCK

    def cur(which):
        return pl.BlockSpec((None, None, tq, w), lambda bb, c, i: (bb, c, i, which))

    def prev(which):
        return pl.BlockSpec((None, None, ATTN_BLOCK, w),
                            lambda bb, c, i: (bb, c, jnp.maximum(i * sub - 1, 0), which))

    return pl.pallas_call(
        functools.partial(_dil_kernel, tq=tq),
        grid=(b, dil, n // tq),
        in_specs=[cur(0), cur(1), cur(2), prev(1), prev(2),
                  pl.BlockSpec((None, N_HEADS, ATTN_BLOCK, 2 * ATTN_BLOCK), lambda bb, c, i: (g, 0, 0, 0))],
        out_specs=[pl.BlockSpec((None, None, tq, w), lambda bb, c, i: (bb, c, i, 0)),
                   pl.BlockSpec((None, None, tq, LANES), lambda bb, c, i: (bb, c, i, 0))],
        out_shape=[jax.ShapeDtypeStruct((b, dil, n, w), BF16),
                   jax.ShapeDtypeStruct((b, dil, n, LANES), F32)],
        compiler_params=_cparams(("arbitrary", "arbitrary", "arbitrary")),
        name=f"dilated_attn_g{g}",
    )(qkv, qkv, qkv, qkv, qkv, bias)


def _combine_out_kernel(o0, o1, o2, l0, l1, l2, e_ref, w_ref, g_ref, h_ref, out_ref, o_sc, l_sc):
    tm = h_ref.shape[0]

    def token_order(o_ref, l_ref, slot):
        dil = o_ref.shape[0]
        if dil == 1:
            return o_ref[0].astype(F32), l_ref[0]
        per = tm // dil
        nlb = o_ref.shape[-1] // LANES
        for c in range(dil):
            oc = o_ref[c].astype(F32)
            for j in range(nlb):
                o_sc[slot, j, pl.ds(c, per, stride=dil), :] = oc[:, j * LANES:(j + 1) * LANES]
            l_sc[slot, pl.ds(c, per, stride=dil), :] = l_ref[c]
        return jnp.concatenate([o_sc[slot, j] for j in range(nlb)], axis=1), l_sc[slot]

    pairs = [token_order(o_ref, l_ref, slot) for slot, (o_ref, l_ref) in enumerate(((o0, l0), (o1, l1), (o2, l2)))]
    ls = [l for _, l in pairs]
    mx = jnp.maximum(jnp.maximum(ls[0], ls[1]), ls[2])
    ws = [jnp.exp(l - mx) for l in ls]
    inv = 1.0 / (ws[0] + ws[1] + ws[2])
    a = None
    e = e_ref[...]
    for (og, _), wg in zip(pairs, ws):
        rem = wg * inv
        ax = None
        for _ in range(3):
            piece = rem.astype(BF16)
            rem = rem - piece.astype(F32)
            part = jnp.dot(piece, e, preferred_element_type=F32)
            ax = part if ax is None else ax + part
        t = ax * og
        a = t if a is None else a + t
    y = jnp.dot(a.astype(BF16), w_ref[...], preferred_element_type=F32)
    out_ref[...] = h_ref[...] + _rms(y, g_ref[...], RMS_EPS)


def _combine_out(os_, lses, w_bf16, g, h2d, b, s, tm=512):
    m, d = h2d.shape
    w = os_[0].shape[-1]
    tps = s // tm
    expand = np.zeros((LANES, w), np.float32)
    for hh in range(N_HEADS):
        expand[hh, hh * HEAD_DIM:(hh + 1) * HEAD_DIM] = 1.0
    grp = lambda a: pl.BlockSpec((None, a.shape[1], tm // a.shape[1], a.shape[3]), lambda bb, i: (bb, 0, i, 0))
    row = pl.BlockSpec((tm, d), lambda bb, i: (bb * tps + i, 0))
    full = lambda shp: pl.BlockSpec(shp, lambda bb, i: (0, 0))
    return pl.pallas_call(
        _combine_out_kernel,
        grid=(b, tps),
        in_specs=[grp(a) for a in os_] + [grp(a) for a in lses]
                 + [full((LANES, w)), full((w, d)), full((1, d)), row],
        out_specs=row,
        out_shape=jax.ShapeDtypeStruct((m, d), F32),
        scratch_shapes=[pltpu.VMEM((3, w // LANES, tm, LANES), F32), pltpu.VMEM((3, tm, LANES), F32)],
        compiler_params=_cparams(("arbitrary", "arbitrary")),
        name="combine_out_proj",
    )(*os_, *lses, jnp.asarray(expand, BF16), w_bf16, g.reshape(1, d), h2d)


def _out_kernel(a_ref, w_ref, g_ref, h_ref, out_ref):
    y = jnp.dot(a_ref[...], w_ref[...], preferred_element_type=F32)
    out_ref[...] = h_ref[...] + _rms(y, g_ref[...], RMS_EPS)


def _out_proj(a2d, w_bf16, g, h2d, tm=512):
    m, d = h2d.shape
    k = a2d.shape[1]
    return pl.pallas_call(
        _out_kernel,
        grid=(m // tm,),
        in_specs=[pl.BlockSpec((tm, k), lambda i: (i, 0)),
                  pl.BlockSpec((k, d), lambda i: (0, 0)),
                  pl.BlockSpec((1, d), lambda i: (0, 0)),
                  pl.BlockSpec((tm, d), lambda i: (i, 0))],
        out_specs=pl.BlockSpec((tm, d), lambda i: (i, 0)),
        out_shape=jax.ShapeDtypeStruct((m, d), F32),
        compiler_params=_cparams(("arbitrary",)),
        name="out_proj",
    )(a2d, w_bf16, g.reshape(1, d), h2d)


def _ffn_kernel(h_ref, g_in_ref, wup_ref, cw_ref, cb_ref, wdn_ref, g_out_ref, out_ref,
                xn_sc, halo_sc, acc_sc, *u_bufs, tiles_per_seq):
    tm = h_ref.shape[0]
    nc = wdn_ref.shape[0]
    i = pl.program_id(0)
    xn_sc[...] = _rms(h_ref[...], g_in_ref[...], RMS_EPS).astype(BF16)
    acc_sc[...] = jnp.zeros(acc_sc.shape, F32)
    seq_start = (i % tiles_per_seq) == 0
    inv_sqrt2 = 1.0 / math.sqrt(2.0)

    def up(c, u_sc):
        u_sc[0, 8:tm + 8, :] = jnp.dot(xn_sc[...], wup_ref[c], preferred_element_type=F32)
        u_sc[1, 8:tm + 8, :] = jnp.dot(xn_sc[...], wup_ref[nc + c], preferred_element_type=F32)

    def conv(u_sc, j, cc):
        cw = cw_ref[cc]
        return (cb_ref[cc] + cw[0:1, :] * u_sc[j, 6:tm + 6, :] + cw[1:2, :] * u_sc[j, 7:tm + 7, :]
                + cw[2:3, :] * u_sc[j, 8:tm + 8, :])

    def down(c, u_sc):
        u_sc[:, 0:8, :] = jnp.where(seq_start, 0.0, halo_sc[c])
        halo_sc[c] = u_sc[:, tm:tm + 8, :]
        gate = conv(u_sc, 0, c)
        val = conv(u_sc, 1, nc + c)
        act = (0.5 * gate * (1.0 + lax.erf(gate * inv_sqrt2)) * val).astype(BF16)
        acc_sc[...] += jnp.dot(act, wdn_ref[c], preferred_element_type=F32)

    for c in range(min(FFN_AHEAD, nc)):
        up(c, u_bufs[c])
    for c in range(nc):
        if c + FFN_AHEAD < nc:
            up(c + FFN_AHEAD, u_bufs[(c + FFN_AHEAD) % (FFN_AHEAD + 1)])
        down(c, u_bufs[c % (FFN_AHEAD + 1)])
    out_ref[...] = h_ref[...] + _rms(acc_sc[...], g_out_ref[...], RMS_EPS)


def _conv_ffn(h2d, g_in, w_up, conv_w, conv_b, w_down, g_out, seq, tm=512, f_chunk=256):
    m, d = h2d.shape
    f2 = w_up.shape[1]
    d_ff = f2 // 2
    assert seq % tm == 0 and d_ff % f_chunk == 0
    nc = d_ff // f_chunk
    wup_c = w_up.astype(BF16).reshape(d, 2 * nc, f_chunk).transpose(1, 0, 2)
    wdn_c = w_down.astype(BF16).reshape(nc, f_chunk, d)
    cw_c = conv_w.reshape(CONV_WIDTH, 2 * nc, f_chunk).transpose(1, 0, 2)
    cb_c = conv_b.reshape(2 * nc, 1, f_chunk)
    full = lambda a: pl.BlockSpec(a.shape, lambda i: (0,) * a.ndim)
    g_in2, g_out2 = g_in.reshape(1, d), g_out.reshape(1, d)
    return pl.pallas_call(
        functools.partial(_ffn_kernel, tiles_per_seq=seq // tm),
        grid=(m // tm,),
        in_specs=[pl.BlockSpec((tm, d), lambda i: (i, 0)),
                  full(g_in2), full(wup_c), full(cw_c), full(cb_c), full(wdn_c), full(g_out2)],
        out_specs=pl.BlockSpec((tm, d), lambda i: (i, 0)),
        out_shape=jax.ShapeDtypeStruct((m, d), F32),
        scratch_shapes=[pltpu.VMEM((tm, d), BF16), pltpu.VMEM((nc, 2, 8, f_chunk), F32), pltpu.VMEM((tm, d), F32)]
                       + [pltpu.VMEM((2, tm + 8, f_chunk), F32)] * (FFN_AHEAD + 1),
        compiler_params=_cparams(("arbitrary",)),
        name="conv_ffn",
    )(h2d, g_in2, wup_c, cw_c, cb_c, wdn_c, g_out2)


def _diff_kernel(q_ref, k_ref, v_ref, bias_ref, lq1, lk1, lq2, lk2, sg_ref, o_ref,
                 qs_sc, s_sc, m_sc, l_sc, acc_sc, *, tile, n_bias, lambda_init, heads):
    qi = pl.program_id(2)
    hd2 = 2 * HEAD_DIM
    lane = lax.broadcasted_iota(jnp.int32, (1, LANES), 1)
    scale = jnp.asarray(HEAD_DIM ** -0.5, q_ref.dtype)
    for hh in range(heads):
        q = q_ref[:, hh * hd2:(hh + 1) * hd2]
        zero = jnp.zeros_like(q)
        qs_sc[hh, 0:tile, :] = jnp.where(lane < HEAD_DIM, q, zero) * scale
        qs_sc[hh, tile:2 * tile, :] = jnp.where(lane >= HEAD_DIM, q, zero) * scale
    m_sc[...] = jnp.full(m_sc.shape, NEG_INF, F32)
    l_sc[...] = jnp.zeros(l_sc.shape, F32)
    acc_sc[...] = jnp.zeros(acc_sc.shape, F32)
    ones = jnp.ones((tile, LANES), BF16)

    def scores(ki, hh, c):
        k = k_ref[pl.ds(pl.multiple_of(ki * tile, tile), tile), hh * hd2:(hh + 1) * hd2]
        return lax.dot_general(qs_sc[hh, c * tile:(c + 1) * tile, :], k, (((1,), (1,)), ((), ())),
                               preferred_element_type=F32)

    def step(ki, prefetch):
        start = pl.multiple_of(ki * tile, tile)
        d = jnp.minimum(qi - ki, n_bias - 1)
        for hh in range(heads):
            vx = jnp.concatenate([v_ref[pl.ds(start, tile), hh * hd2:(hh + 1) * hd2], ones], axis=1)
            for c in range(2):
                rows = slice(c * tile, (c + 1) * tile)
                s = s_sc[hh, rows, :] + bias_ref[hh, d]
                if prefetch:
                    s_sc[hh, rows, :] = scores(ki + 1, hh, c)
                m_prev = m_sc[hh, rows, :]
                m_new = jnp.maximum(m_prev, jnp.max(s, axis=-1, keepdims=True))
                alpha = jnp.exp(m_prev - m_new)
                p = jnp.exp(s - jnp.concatenate([m_new] * (tile // LANES), axis=1))
                pv = jnp.dot(p.astype(BF16), vx, preferred_element_type=F32)
                l_sc[hh, rows, :] = alpha * l_sc[hh, rows, :] + pv[:, hd2:]
                acc_sc[hh, rows, :] = alpha * acc_sc[hh, rows, :] + pv[:, :hd2]
                m_sc[hh, rows, :] = m_new

    for hh in range(heads):
        for c in range(2):
            s_sc[hh, c * tile:(c + 1) * tile, :] = scores(0, hh, c)

    def body(ki, carry):
        step(ki, True)
        return carry

    lax.fori_loop(0, qi, body, 0)
    step(qi, False)

    lam = (jnp.exp(jnp.sum(lq1[...].astype(F32) * lk1[...].astype(F32), axis=-1, keepdims=True))
           - jnp.exp(jnp.sum(lq2[...].astype(F32) * lk2[...].astype(F32), axis=-1, keepdims=True))
           + lambda_init)
    for hh in range(heads):
        o = acc_sc[hh] * (1.0 / l_sc[hh])
        a = o[0:tile, :] - lam * o[tile:2 * tile, :]
        y = _rms(a, sg_ref[...], SUBLN_EPS) * (1.0 - lambda_init)
        o_ref[:, hh * hd2:(hh + 1) * hd2] = y.astype(o_ref.dtype)


def _diff_attention(q, kv, bias, lq1, lk1, lq2, lk2, subln_g, lambda_init, tile, heads=2):
    b, s, _ = q.shape
    n_bias = bias.shape[1]
    hd2 = 2 * HEAD_DIM
    hw = heads * hd2
    ng = N_HEADS // heads
    vec = lambda a: a.reshape(1, -1)
    small = lambda n: pl.BlockSpec((1, n), lambda h, bb, i: (0, 0))
    return pl.pallas_call(
        functools.partial(_diff_kernel, tile=tile, n_bias=n_bias, lambda_init=lambda_init, heads=heads),
        grid=(ng, b, s // tile),
        in_specs=[pl.BlockSpec((None, tile, hw), lambda h, bb, i: (bb, i, h)),
                  pl.BlockSpec((None, s, hw), lambda h, bb, i: (bb, 0, h)),
                  pl.BlockSpec((None, s, hw), lambda h, bb, i: (bb, 0, ng + h)),
                  pl.BlockSpec((heads, n_bias, tile, tile), lambda h, bb, i: (h, 0, 0, 0)),
                  small(HEAD_DIM), small(HEAD_DIM), small(HEAD_DIM), small(HEAD_DIM), small(hd2)],
        out_specs=pl.BlockSpec((None, tile, hw), lambda h, bb, i: (bb, i, h)),
        out_shape=jax.ShapeDtypeStruct((b, s, N_HEADS * hd2), BF16),
        scratch_shapes=[pltpu.VMEM((heads, 2 * tile, hd2), BF16), pltpu.VMEM((heads, 2 * tile, tile), F32),
                        pltpu.VMEM((heads, 2 * tile, LANES), F32), pltpu.VMEM((heads, 2 * tile, LANES), F32),
                        pltpu.VMEM((heads, 2 * tile, hd2), F32)],
        compiler_params=_cparams(("arbitrary", "arbitrary", "arbitrary")),
        name="diff_attn",
    )(q, kv, kv, bias, vec(lq1), vec(lk1), vec(lq2), vec(lk2), vec(subln_g))


DIFF_TILE = 512
FFN_AHEAD = 6


def kernel(x, rel_bias_table, norm_g, w_in_a, w_out_a, kv_norm_g, w_k_shared, w_v_shared, w_q_b,
           lam_q1, lam_k1, lam_q2, lam_k2, subln_g, w_out_b, w_up, conv_w, conv_b, w_down):
    b, s, d = x.shape
    depth = norm_g.shape[0]
    n_a = w_in_a.shape[0]
    bf = lambda a: a.astype(BF16)

    dil_bias = _bias_tiles(rel_bias_table, _dilated_bias_idx(), head_major=False)
    diff_tile = min(DIFF_TILE, s)
    diff_bias = _bias_tiles(rel_bias_table, _diff_bias_idx(s, diff_tile), head_major=True)

    h = x.reshape(b * s, d)
    kv = None
    for layer in range(depth):
        g = norm_g[layer]
        if layer < n_a:
            qkvs = _proj_a(h, g[0], bf(w_in_a[layer]), b, s)
            outs = [_dilated_group(qkvs[gi], dil_bias, gi, dil) for gi, (_, dil) in enumerate(DIL_GROUPS)]
            h = _combine_out([o for o, _ in outs], [l for _, l in outs], bf(w_out_a[layer]), g[1], h, b, s)
        else:
            j = layer - n_a
            lambda_init = 0.8 - 0.6 * math.exp(-0.3 * layer)
            q = _norm_matmul(h, g[0], bf(w_q_b[j])).reshape(b, s, -1)
            a = _diff_attention(q, kv, diff_bias, lam_q1[j], lam_k1[j], lam_q2[j], lam_k2[j],
                                subln_g[j], lambda_init, diff_tile)
            h = _out_proj(a.reshape(b * s, -1), bf(w_out_b[j]), g[1], h)
        h = _conv_ffn(h, g[2], w_up[layer], conv_w[layer], conv_b[layer], w_down[layer], g[3], s)
        if layer == n_a - 1:
            w_kv = jnp.concatenate([w_k_shared, w_v_shared], axis=1)
            kv = _norm_matmul(h, kv_norm_g, bf(w_kv)).reshape(b, s, -1)
    return h.reshape(b, s, d)
```

```python
import functools
import math

import numpy as np
import jax
import jax.numpy as jnp
from jax import lax
from jax.experimental import pallas as pl
from jax.experimental.pallas import tpu as pltpu

F32 = jnp.float32
BF16 = jnp.bfloat16

HEAD_DIM = 64
N_HEADS = 8
DIL_GROUPS = ((128, 1), (512, 4), (2048, 16))
ATTN_BLOCK = 128
NUM_BUCKETS = 32
MAX_DISTANCE = 2048
CONV_WIDTH = 3
RMS_EPS = 1e-6
SUBLN_EPS = 1e-5

LANES = 128
BIAS_ROWS = 32
VMEM_LIMIT = 56 * 1024 * 1024

NEG_INF = float("-inf")
LOG2E = math.log2(math.e)


def _cparams(sem):
    return pltpu.CompilerParams(dimension_semantics=sem, vmem_limit_bytes=VMEM_LIMIT)


def _bucket_np(dist):
    n = np.maximum(dist, 0)
    max_exact = NUM_BUCKETS // 2

    def large(dtype):
        nf = np.maximum(n, 1).astype(dtype)
        v = np.log(nf / dtype(max_exact)) / dtype(math.log(MAX_DISTANCE / max_exact)) * dtype(NUM_BUCKETS - max_exact)
        return np.minimum(max_exact + v.astype(np.int32), NUM_BUCKETS - 1)

    l32, l64 = large(np.float32), large(np.float64)
    assert np.array_equal(l32, l64), "bucket boundary is rounding sensitive"
    return np.where(n < max_exact, n, l32).astype(np.int32)


def _bias_kernel(tbl_ref, idx_ref, o_ref, *, plan, scale):
    h = pl.program_id(0)
    for t, blocks in enumerate(plan):
        for rb, (buckets, has_mask) in enumerate(blocks):
            rows = slice(rb * BIAS_ROWS, (rb + 1) * BIAS_ROWS)
            shape = (BIAS_ROWS, idx_ref.shape[-1])
            if len(buckets) == 1 and not has_mask:
                o_ref[t, rows, :] = jnp.full(shape, tbl_ref[h, buckets[0]] * scale, F32)
                continue
            idx = idx_ref[t, rows, :]
            acc = jnp.full(shape, NEG_INF, F32)
            for k in buckets:
                acc = jnp.where(idx == k, tbl_ref[h, k] * scale, acc)
            o_ref[t, rows, :] = acc


def _bias_tiles(table, idx, head_major, scale=1.0):
    t, r, c = idx.shape
    nh = table.shape[0]
    plan = tuple(
        tuple((tuple(int(k) for k in np.unique(blk[blk >= 0])), bool((blk < 0).any()))
              for blk in idx[i].reshape(r // BIAS_ROWS, BIAS_ROWS * c))
        for i in range(t))
    if head_major:
        out_shape, out_block, out_map = (nh, t, r, c), (None, t, r, c), (lambda h: (h, 0, 0, 0))
    else:
        out_shape, out_block, out_map = (t, nh, r, c), (t, None, r, c), (lambda h: (0, h, 0, 0))
    return pl.pallas_call(
        functools.partial(_bias_kernel, plan=plan, scale=scale),
        grid=(nh,),
        in_specs=[pl.BlockSpec(memory_space=pltpu.SMEM), pl.BlockSpec((t, r, c), lambda h: (0, 0, 0))],
        out_specs=pl.BlockSpec(out_block, out_map),
        out_shape=jax.ShapeDtypeStruct(out_shape, F32),
        compiler_params=_cparams(("arbitrary",)),
        name="rel_bias_tiles",
    )(table, jnp.asarray(idx))


def _dilated_bias_idx():
    qi = np.arange(ATTN_BLOCK)[:, None]
    ki = np.arange(2 * ATTN_BLOCK)[None, :]
    dist_u = qi + ATTN_BLOCK - ki
    out = []
    for window, dil in DIL_GROUPS:
        band = (dist_u >= 0) & (dist_u <= window // dil)
        out.append(np.where(band, _bucket_np(dist_u * dil), -1))
    return np.stack(out).astype(np.int32)


def _diff_bias_idx(seq, tile):
    nq = seq // tile
    i = np.arange(tile)[:, None]
    j = np.arange(tile)[None, :]
    tiles = []
    for d in range(nq):
        dist = d * tile + i - j
        tiles.append(np.where(dist >= 0, _bucket_np(dist), -1).astype(np.int32))
    nd = nq
    while nd > 1 and np.array_equal(tiles[nd - 1], tiles[nd - 2]):
        nd -= 1
    assert all(np.array_equal(tiles[d], tiles[nd - 1]) for d in range(nd - 1, nq))
    return np.stack(tiles[:nd])


def _rms(x, g, eps):
    return x * lax.rsqrt(jnp.mean(x * x, axis=-1, keepdims=True) + eps) * g


def _norm_matmul_kernel(x_ref, *refs, n_chunk):
    n_proj = len(refs) // 3
    x = x_ref[...]
    xs = x * lax.rsqrt(jnp.mean(x * x, axis=-1, keepdims=True) + RMS_EPS)
    for p in range(n_proj):
        g_ref, w_ref, o_ref = refs[2 * p], refs[2 * p + 1], refs[2 * n_proj + p]
        xn = (xs * g_ref[...]).astype(BF16)
        for j in range(o_ref.shape[-1] // n_chunk):
            sl = slice(j * n_chunk, (j + 1) * n_chunk)
            o_ref[:, sl] = jnp.dot(xn, w_ref[:, sl], preferred_element_type=F32).astype(o_ref.dtype)


def _norm_matmul(x2d, projs, tm=512, n_chunk=512):
    m, d = x2d.shape
    ins, in_specs = [], []
    for g, w in projs:
        ins += [g.reshape(1, d), w]
        in_specs += [pl.BlockSpec((1, d), lambda i: (0, 0)), pl.BlockSpec(w.shape, lambda i: (0, 0))]
    return pl.pallas_call(
        functools.partial(_norm_matmul_kernel, n_chunk=n_chunk),
        grid=(m // tm,),
        in_specs=[pl.BlockSpec((tm, d), lambda i: (i, 0))] + in_specs,
        out_specs=[pl.BlockSpec((tm, w.shape[1]), lambda i: (i, 0)) for _, w in projs],
        out_shape=[jax.ShapeDtypeStruct((m, w.shape[1]), BF16) for _, w in projs],
        compiler_params=_cparams(("arbitrary",)),
        name="norm_matmul",
    )(x2d, *ins)


def _proj_a_kernel(x_ref, g_ref, w_ref, *refs, n_chunk):
    out_refs, xn_sc = refs[:-1], refs[-1]
    tm = x_ref.shape[0]
    xn = _rms(x_ref[...], g_ref[...], RMS_EPS)
    nlb = xn.shape[1] // LANES
    for j in range(nlb):
        xn_sc[j] = xn[:, j * LANES:(j + 1) * LANES]
    gw = out_refs[0].shape[-1]
    for gi, (o_ref, (_, dil)) in enumerate(zip(out_refs, DIL_GROUPS)):
        per = tm // dil
        if dil == 1:
            xg = xn
        else:
            xg = jnp.concatenate(
                [jnp.concatenate([xn_sc[j, pl.ds(c, per, stride=dil), :] for c in range(dil)], axis=0)
                 for j in range(nlb)], axis=1)
        xg = xg.astype(BF16)
        for j in range(gw // n_chunk):
            res = jnp.dot(xg, w_ref[:, gi * gw + j * n_chunk:gi * gw + (j + 1) * n_chunk],
                          preferred_element_type=F32)
            if j == 0:
                res = res * (HEAD_DIM ** -0.5 * LOG2E)
            res = res.astype(o_ref.dtype)
            for c in range(dil):
                o_ref[c, :, j * n_chunk:(j + 1) * n_chunk] = res[c * per:(c + 1) * per, :]


def _proj_a(x2d, g, w_bf16, b, s, tm=512, n_chunk=512):
    m, d = x2d.shape
    gw = w_bf16.shape[1] // len(DIL_GROUPS)
    tps = s // tm
    assert all(tm % (16 * dil) == 0 for _, dil in DIL_GROUPS) and n_chunk == N_HEADS * HEAD_DIM
    return pl.pallas_call(
        functools.partial(_proj_a_kernel, n_chunk=n_chunk),
        grid=(b, tps),
        in_specs=[pl.BlockSpec((tm, d), lambda bb, i: (bb * tps + i, 0)),
                  pl.BlockSpec((1, d), lambda bb, i: (0, 0)),
                  pl.BlockSpec(w_bf16.shape, lambda bb, i: (0, 0))],
        out_specs=[pl.BlockSpec((None, dil, tm // dil, gw), lambda bb, i: (bb, 0, i, 0)) for _, dil in DIL_GROUPS],
        out_shape=[jax.ShapeDtypeStruct((b, dil, s // dil, gw), BF16) for _, dil in DIL_GROUPS],
        scratch_shapes=[pltpu.VMEM((d // LANES, tm, LANES), F32)],
        compiler_params=_cparams(("arbitrary", "arbitrary")),
        name="proj_dilated",
    )(x2d, g.reshape(1, d), w_bf16)


def _dil_kernel(q_ref, k_ref, v_ref, kp_ref, vp_ref, bias_ref, o_ref, lse_ref, *, tq):
    blk = ATTN_BLOCK
    i = pl.program_id(2)
    lane = lax.broadcasted_iota(jnp.int32, (1, LANES), 1)
    col = lax.broadcasted_iota(jnp.int32, (blk, 2 * blk), 1)
    first_pen = jnp.where(jnp.logical_and(i == 0, col < blk), NEG_INF, 0.0).astype(F32)
    ones = jnp.ones((2 * blk, LANES), BF16)
    for j in range(tq // blk):
        rows = slice(j * blk, (j + 1) * blk)
        qj = q_ref[rows, :]
        if j == 0:
            kx = jnp.concatenate([kp_ref[...], k_ref[0:blk, :]], axis=0)
            vx = jnp.concatenate([vp_ref[...], v_ref[0:blk, :]], axis=0)
        else:
            kx = k_ref[(j - 1) * blk:(j + 1) * blk, :]
            vx = v_ref[(j - 1) * blk:(j + 1) * blk, :]
        lse_tile = jnp.zeros((blk, LANES), F32)
        for hp in range(N_HEADS // 2):
            cols = slice(hp * LANES, (hp + 1) * LANES)
            q2, k2 = qj[:, cols], kx[:, cols]
            v2 = jnp.concatenate([vx[:, cols], ones], axis=1)
            o_pair = None
            for e in range(2):
                h = 2 * hp + e
                sel = (lane >= HEAD_DIM) if e else (lane < HEAD_DIM)
                qm = jnp.where(sel, q2, jnp.zeros_like(q2))
                s = lax.dot_general(qm, k2, (((1,), (1,)), ((), ())), preferred_element_type=F32) + bias_ref[h]
                if j == 0:
                    s = s + first_pen
                m = jnp.max(s, axis=-1, keepdims=True)
                p = jnp.exp2(s - m)
                pv = jnp.dot(p.astype(BF16), v2, preferred_element_type=F32)
                l = pv[:, LANES:]
                o_e = pv[:, :LANES] * (1.0 / l)
                o_pair = o_e if e == 0 else jnp.where(lane < HEAD_DIM, o_pair, o_e)
                lse_tile = jnp.where(lane == h, m + jnp.log2(l), lse_tile)
            o_ref[rows, cols] = o_pair.astype(o_ref.dtype)
        lse_ref[rows, :] = lse_tile


def _dilated_group(qkv, bias, g, dil):
    b, r, n, _ = qkv.shape
    w = N_HEADS * HEAD_DIM
    assert r == dil and n % ATTN_BLOCK == 0
    tq = min(n, 2 * ATTN_BLOCK)
    sub = tq // ATTN_BLOCK

    def cur(which):
        return pl.BlockSpec((None, None, tq, w), lambda bb, c, i: (bb, c, i, which))

    def prev(which):
        return pl.BlockSpec((None, None, ATTN_BLOCK, w),
                            lambda bb, c, i: (bb, c, jnp.maximum(i * sub - 1, 0), which))

    return pl.pallas_call(
        functools.partial(_dil_kernel, tq=tq),
        grid=(b, dil, n // tq),
        in_specs=[cur(0), cur(1), cur(2), prev(1), prev(2),
                  pl.BlockSpec((None, N_HEADS, ATTN_BLOCK, 2 * ATTN_BLOCK), lambda bb, c, i: (g, 0, 0, 0))],
        out_specs=[pl.BlockSpec((None, None, tq, w), lambda bb, c, i: (bb, c, i, 0)),
                   pl.BlockSpec((None, None, tq, LANES), lambda bb, c, i: (bb, c, i, 0))],
        out_shape=[jax.ShapeDtypeStruct((b, dil, n, w), BF16),
                   jax.ShapeDtypeStruct((b, dil, n, LANES), F32)],
        compiler_params=_cparams(("arbitrary", "arbitrary", "arbitrary")),
        name=f"dilated_attn_g{g}",
    )(qkv, qkv, qkv, qkv, qkv, bias)


def _combine_out_kernel(o0, o1, o2, l0, l1, l2, e_ref, w_ref, g_ref, h_ref, out_ref, o_sc, l_sc):
    tm = h_ref.shape[0]

    def token_order(o_ref, l_ref, slot):
        dil = o_ref.shape[0]
        if dil == 1:
            return o_ref[0].astype(F32), l_ref[0]
        per = tm // dil
        nlb = o_ref.shape[-1] // LANES
        for c in range(dil):
            oc = o_ref[c].astype(F32)
            for j in range(nlb):
                o_sc[slot, j, pl.ds(c, per, stride=dil), :] = oc[:, j * LANES:(j + 1) * LANES]
            l_sc[slot, pl.ds(c, per, stride=dil), :] = l_ref[c]
        return jnp.concatenate([o_sc[slot, j] for j in range(nlb)], axis=1), l_sc[slot]

    pairs = [token_order(o_ref, l_ref, slot) for slot, (o_ref, l_ref) in enumerate(((o0, l0), (o1, l1), (o2, l2)))]
    ls = [l for _, l in pairs]
    mx = jnp.maximum(jnp.maximum(ls[0], ls[1]), ls[2])
    ws = [jnp.exp2(l - mx) for l in ls]
    inv = 1.0 / (ws[0] + ws[1] + ws[2])
    a = None
    e = e_ref[...]
    for (og, _), wg in zip(pairs, ws):
        rem = wg * inv
        ax = None
        for _ in range(3):
            piece = rem.astype(BF16)
            rem = rem - piece.astype(F32)
            part = jnp.dot(piece, e, preferred_element_type=F32)
            ax = part if ax is None else ax + part
        t = ax * og
        a = t if a is None else a + t
    y = jnp.dot(a.astype(BF16), w_ref[...], preferred_element_type=F32)
    out_ref[...] = h_ref[...] + _rms(y, g_ref[...], RMS_EPS)


def _combine_out(os_, lses, w_bf16, g, h2d, b, s, tm=512):
    m, d = h2d.shape
    w = os_[0].shape[-1]
    tps = s // tm
    expand = np.zeros((LANES, w), np.float32)
    for hh in range(N_HEADS):
        expand[hh, hh * HEAD_DIM:(hh + 1) * HEAD_DIM] = 1.0
    grp = lambda a: pl.BlockSpec((None, a.shape[1], tm // a.shape[1], a.shape[3]), lambda bb, i: (bb, 0, i, 0))
    row = pl.BlockSpec((tm, d), lambda bb, i: (bb * tps + i, 0))
    full = lambda shp: pl.BlockSpec(shp, lambda bb, i: (0, 0))
    return pl.pallas_call(
        _combine_out_kernel,
        grid=(b, tps),
        in_specs=[grp(a) for a in os_] + [grp(a) for a in lses]
                 + [full((LANES, w)), full((w, d)), full((1, d)), row],
        out_specs=row,
        out_shape=jax.ShapeDtypeStruct((m, d), F32),
        scratch_shapes=[pltpu.VMEM((3, w // LANES, tm, LANES), F32), pltpu.VMEM((3, tm, LANES), F32)],
        compiler_params=_cparams(("arbitrary", "arbitrary")),
        name="combine_out_proj",
    )(*os_, *lses, jnp.asarray(expand, BF16), w_bf16, g.reshape(1, d), h2d)


def _ffn_kernel(*refs, tiles_per_seq, f_chunk, fuse_out_proj):
    if fuse_out_proj:
        a_ref, wo_ref, g_mix_ref, refs = refs[0], refs[1], refs[2], refs[3:]
    h_ref, g_in_ref, wup_ref, cw_ref, cb_ref, wdn_ref, g_out_ref, out_ref, h_sc, xn_sc, halo_sc, acc_sc = refs[:12]
    u_bufs = refs[12:]
    tm = h_ref.shape[0]
    d_ff = wdn_ref.shape[0]
    nc = d_ff // f_chunk
    i = pl.program_id(0)
    if fuse_out_proj:
        mix = jnp.dot(a_ref[...], wo_ref[...], preferred_element_type=F32)
        h_sc[...] = h_ref[...] + _rms(mix, g_mix_ref[...], RMS_EPS)
    else:
        h_sc[...] = h_ref[...]
    xn_sc[...] = _rms(h_sc[...], g_in_ref[...], RMS_EPS).astype(BF16)
    acc_sc[...] = jnp.zeros(acc_sc.shape, F32)
    seq_start = (i % tiles_per_seq) == 0
    inv_sqrt2 = 1.0 / math.sqrt(2.0)

    def cols(c, j):
        return slice(j * d_ff + c * f_chunk, j * d_ff + (c + 1) * f_chunk)

    def up(c, u_sc):
        for j in range(2):
            u_sc[j, 8:tm + 8, :] = jnp.dot(xn_sc[...], wup_ref[:, cols(c, j)], preferred_element_type=F32)

    def conv(u_sc, c, j):
        cw = cw_ref[:, cols(c, j)]
        return (cb_ref[:, cols(c, j)] + cw[0:1, :] * u_sc[j, 6:tm + 6, :] + cw[1:2, :] * u_sc[j, 7:tm + 7, :]
                + cw[2:3, :] * u_sc[j, 8:tm + 8, :])

    def down(c, u_sc):
        u_sc[:, 0:8, :] = jnp.where(seq_start, 0.0, halo_sc[c])
        halo_sc[c] = u_sc[:, tm:tm + 8, :]
        gate = conv(u_sc, c, 0)
        val = conv(u_sc, c, 1)
        act = (0.5 * gate * (1.0 + lax.erf(gate * inv_sqrt2)) * val).astype(BF16)
        acc_sc[...] += jnp.dot(act, wdn_ref[c * f_chunk:(c + 1) * f_chunk, :], preferred_element_type=F32)

    for c in range(min(FFN_AHEAD, nc)):
        up(c, u_bufs[c])
    for c in range(nc):
        if c + FFN_AHEAD < nc:
            up(c + FFN_AHEAD, u_bufs[(c + FFN_AHEAD) % (FFN_AHEAD + 1)])
        down(c, u_bufs[c % (FFN_AHEAD + 1)])
    out_ref[...] = h_sc[...] + _rms(acc_sc[...], g_out_ref[...], RMS_EPS)


def _conv_ffn(h2d, g_in, w_up_bf16, conv_w, conv_b, w_down_bf16, g_out, seq, out_proj=None, tm=512, f_chunk=256):
    m, d = h2d.shape
    f2 = w_up_bf16.shape[1]
    d_ff = f2 // 2
    assert seq % tm == 0 and d_ff % f_chunk == 0
    nc = d_ff // f_chunk
    full = lambda shp: pl.BlockSpec(shp, lambda i: (0, 0))
    row = lambda width: pl.BlockSpec((tm, width), lambda i: (i, 0))
    ins = [h2d, g_in.reshape(1, d), w_up_bf16, conv_w, conv_b.reshape(1, f2), w_down_bf16, g_out.reshape(1, d)]
    in_specs = [row(d), full((1, d)), full((d, f2)), full((CONV_WIDTH, f2)), full((1, f2)), full((d_ff, d)),
                full((1, d))]
    if out_proj is not None:
        a2d, w_out_bf16, g_mix = out_proj
        ins = [a2d, w_out_bf16, g_mix.reshape(1, d)] + ins
        in_specs = [row(a2d.shape[1]), full(w_out_bf16.shape), full((1, d))] + in_specs
    return pl.pallas_call(
        functools.partial(_ffn_kernel, tiles_per_seq=seq // tm, f_chunk=f_chunk, fuse_out_proj=out_proj is not None),
        grid=(m // tm,),
        in_specs=in_specs,
        out_specs=row(d),
        out_shape=jax.ShapeDtypeStruct((m, d), F32),
        scratch_shapes=[pltpu.VMEM((tm, d), F32), pltpu.VMEM((tm, d), BF16), pltpu.VMEM((nc, 2, 8, f_chunk), F32),
                        pltpu.VMEM((tm, d), F32)] + [pltpu.VMEM((2, tm + 8, f_chunk), F32)] * (FFN_AHEAD + 1),
        compiler_params=_cparams(("arbitrary",)),
        name="conv_ffn",
    )(*ins)


def _diff_kernel(q_ref, k_ref, v_ref, bias_ref, lq1, lk1, lq2, lk2, sg_ref, o_ref,
                 qs_sc, s_sc, m_sc, l_sc, acc_sc, *, tile, n_bias, lambda_init, heads):
    qi = pl.program_id(2)
    hd2 = 2 * HEAD_DIM
    lane = lax.broadcasted_iota(jnp.int32, (1, LANES), 1)
    scale = jnp.asarray(HEAD_DIM ** -0.5, q_ref.dtype)
    for hh in range(heads):
        q = q_ref[:, hh * hd2:(hh + 1) * hd2]
        zero = jnp.zeros_like(q)
        qs_sc[hh, 0:tile, :] = jnp.where(lane < HEAD_DIM, q, zero) * scale
        qs_sc[hh, tile:2 * tile, :] = jnp.where(lane >= HEAD_DIM, q, zero) * scale
    m_sc[...] = jnp.full(m_sc.shape, NEG_INF, F32)
    l_sc[...] = jnp.zeros(l_sc.shape, F32)
    acc_sc[...] = jnp.zeros(acc_sc.shape, F32)
    ones = jnp.ones((tile, LANES), BF16)

    def scores(ki, hh, c):
        k = k_ref[pl.ds(pl.multiple_of(ki * tile, tile), tile), hh * hd2:(hh + 1) * hd2]
        return lax.dot_general(qs_sc[hh, c * tile:(c + 1) * tile, :], k, (((1,), (1,)), ((), ())),
                               preferred_element_type=F32)

    def update(hh, rows, s, vx):
        m_prev = m_sc[hh, rows, :]
        m_new = jnp.maximum(m_prev, jnp.max(s, axis=-1, keepdims=True))
        alpha = jnp.exp(m_prev - m_new)
        p = jnp.exp(s - jnp.concatenate([m_new] * (s.shape[1] // LANES), axis=1))
        pv = jnp.dot(p.astype(BF16), vx, preferred_element_type=F32)
        l_sc[hh, rows, :] = alpha * l_sc[hh, rows, :] + pv[:, hd2:]
        acc_sc[hh, rows, :] = alpha * acc_sc[hh, rows, :] + pv[:, :hd2]
        m_sc[hh, rows, :] = m_new

    def values(ki, hh):
        v = v_ref[pl.ds(pl.multiple_of(ki * tile, tile), tile), hh * hd2:(hh + 1) * hd2]
        return jnp.concatenate([v, ones], axis=1)

    def step(ki):
        d = jnp.minimum(qi - ki, n_bias - 1)
        for hh in range(heads):
            vx = values(ki, hh)
            for c in range(2):
                rows = slice(c * tile, (c + 1) * tile)
                s = s_sc[hh, rows, :] + bias_ref[hh, d]
                s_sc[hh, rows, :] = scores(ki + 1, hh, c)
                update(hh, rows, s, vx)

    def diagonal_step():
        half = tile // 2
        for hh in range(heads):
            vx = values(qi, hh)
            for c in range(2):
                for r0 in range(0, tile, half):
                    rows = slice(c * tile + r0, c * tile + r0 + half)
                    nk = r0 + half
                    s = s_sc[hh, rows, 0:nk] + bias_ref[hh, 0, r0:r0 + half, 0:nk]
                    update(hh, rows, s, vx[0:nk, :])

    for hh in range(heads):
        for c in range(2):
            s_sc[hh, c * tile:(c + 1) * tile, :] = scores(0, hh, c)

    def body(ki, carry):
        step(ki)
        return carry

    lax.fori_loop(0, qi, body, 0)
    diagonal_step()

    lam = (jnp.exp(jnp.sum(lq1[...].astype(F32) * lk1[...].astype(F32), axis=-1, keepdims=True))
           - jnp.exp(jnp.sum(lq2[...].astype(F32) * lk2[...].astype(F32), axis=-1, keepdims=True))
           + lambda_init)
    for hh in range(heads):
        o = acc_sc[hh] * (1.0 / l_sc[hh])
        a = o[0:tile, :] - lam * o[tile:2 * tile, :]
        y = _rms(a, sg_ref[...], SUBLN_EPS) * (1.0 - lambda_init)
        o_ref[:, hh * hd2:(hh + 1) * hd2] = y.astype(o_ref.dtype)


def _diff_attention(q, kv, bias, lq1, lk1, lq2, lk2, subln_g, lambda_init, tile, heads=2):
    b, s, _ = q.shape
    n_bias = bias.shape[1]
    hd2 = 2 * HEAD_DIM
    hw = heads * hd2
    ng = N_HEADS // heads
    vec = lambda a: a.reshape(1, -1)
    small = lambda n: pl.BlockSpec((1, n), lambda h, bb, i: (0, 0))
    return pl.pallas_call(
        functools.partial(_diff_kernel, tile=tile, n_bias=n_bias, lambda_init=lambda_init, heads=heads),
        grid=(ng, b, s // tile),
        in_specs=[pl.BlockSpec((None, tile, hw), lambda h, bb, i: (bb, i, h)),
                  pl.BlockSpec((None, s, hw), lambda h, bb, i: (bb, 0, h)),
                  pl.BlockSpec((None, s, hw), lambda h, bb, i: (bb, 0, ng + h)),
                  pl.BlockSpec((heads, n_bias, tile, tile), lambda h, bb, i: (h, 0, 0, 0)),
                  small(HEAD_DIM), small(HEAD_DIM), small(HEAD_DIM), small(HEAD_DIM), small(hd2)],
        out_specs=pl.BlockSpec((None, tile, hw), lambda h, bb, i: (bb, i, h)),
        out_shape=jax.ShapeDtypeStruct((b, s, N_HEADS * hd2), BF16),
        scratch_shapes=[pltpu.VMEM((heads, 2 * tile, hd2), BF16), pltpu.VMEM((heads, 2 * tile, tile), F32),
                        pltpu.VMEM((heads, 2 * tile, LANES), F32), pltpu.VMEM((heads, 2 * tile, LANES), F32),
                        pltpu.VMEM((heads, 2 * tile, hd2), F32)],
        compiler_params=_cparams(("arbitrary", "arbitrary", "arbitrary")),
        name="diff_attn",
    )(q, kv, kv, bias, vec(lq1), vec(lk1), vec(lq2), vec(lk2), vec(subln_g))


DIFF_TILE = 512
FFN_AHEAD = 6


def kernel(x, rel_bias_table, norm_g, w_in_a, w_out_a, kv_norm_g, w_k_shared, w_v_shared, w_q_b,
           lam_q1, lam_k1, lam_q2, lam_k2, subln_g, w_out_b, w_up, conv_w, conv_b, w_down):
    b, s, d = x.shape
    depth = norm_g.shape[0]
    n_a = w_in_a.shape[0]
    bf = lambda a: a.astype(BF16)

    dil_bias = _bias_tiles(rel_bias_table, _dilated_bias_idx(), head_major=False, scale=LOG2E)
    diff_tile = min(DIFF_TILE, s)
    diff_bias = _bias_tiles(rel_bias_table, _diff_bias_idx(s, diff_tile), head_major=True)

    h = x.reshape(b * s, d)
    kv = q = None
    for layer in range(depth):
        g = norm_g[layer]
        if layer < n_a:
            qkvs = _proj_a(h, g[0], bf(w_in_a[layer]), b, s)
            outs = [_dilated_group(qkvs[gi], dil_bias, gi, dil) for gi, (_, dil) in enumerate(DIL_GROUPS)]
            h = _combine_out([o for o, _ in outs], [l for _, l in outs], bf(w_out_a[layer]), g[1], h, b, s)
            mixer_out = None
        else:
            j = layer - n_a
            lambda_init = 0.8 - 0.6 * math.exp(-0.3 * layer)
            if q is None:
                q, = _norm_matmul(h, [(g[0], bf(w_q_b[j]))])
            a = _diff_attention(q.reshape(b, s, -1), kv, diff_bias, lam_q1[j], lam_k1[j], lam_q2[j], lam_k2[j],
                                subln_g[j], lambda_init, diff_tile)
            q = None
            mixer_out = (a.reshape(b * s, -1), bf(w_out_b[j]), g[1])
        h = _conv_ffn(h, g[2], bf(w_up[layer]), conv_w[layer], conv_b[layer], bf(w_down[layer]), g[3], s,
                      out_proj=mixer_out)
        if layer == n_a - 1:
            w_kv = jnp.concatenate([w_k_shared, w_v_shared], axis=1)
            projs = [(kv_norm_g, bf(w_kv))]
            if layer + 1 < depth:
                projs.append((norm_g[layer + 1, 0], bf(w_q_b[0])))
            res = _norm_matmul(h, projs)
            kv = res[0].reshape(b, s, -1)
            q = res[1] if len(res) > 1 else None
    return h.reshape(b, s, d)
```

```python
import functools
import math

import numpy as np
import jax
import jax.numpy as jnp
from jax import lax
from jax.experimental import pallas as pl
from jax.experimental.pallas import tpu as pltpu

F32 = jnp.float32
BF16 = jnp.bfloat16

HEAD_DIM = 64
N_HEADS = 8
DIL_GROUPS = ((128, 1), (512, 4), (2048, 16))
ATTN_BLOCK = 128
NUM_BUCKETS = 32
MAX_DISTANCE = 2048
CONV_WIDTH = 3
RMS_EPS = 1e-6
SUBLN_EPS = 1e-5

LANES = 128
SUBLANES = 8
ROW_TILE = 512
BIAS_ROWS = 32
VMEM_LIMIT = 56 * 1024 * 1024

NEG_INF = float("-inf")
LOG2E = math.log2(math.e)


def _cparams(sem):
    return pltpu.CompilerParams(dimension_semantics=sem, vmem_limit_bytes=VMEM_LIMIT)


def _bucket_np(dist):
    n = np.maximum(dist, 0)
    max_exact = NUM_BUCKETS // 2

    def large(dtype):
        nf = np.maximum(n, 1).astype(dtype)
        v = np.log(nf / dtype(max_exact)) / dtype(math.log(MAX_DISTANCE / max_exact)) * dtype(NUM_BUCKETS - max_exact)
        return np.minimum(max_exact + v.astype(np.int32), NUM_BUCKETS - 1)

    l32, l64 = large(np.float32), large(np.float64)
    assert np.array_equal(l32, l64), "bucket boundary is rounding sensitive"
    return np.where(n < max_exact, n, l32).astype(np.int32)


def _bias_kernel(tbl_ref, idx_ref, o_ref, *, plan, scale):
    h = pl.program_id(0)
    for t, blocks in enumerate(plan):
        for rb, (buckets, has_mask) in enumerate(blocks):
            rows = slice(rb * BIAS_ROWS, (rb + 1) * BIAS_ROWS)
            shape = (BIAS_ROWS, idx_ref.shape[-1])
            if len(buckets) == 1 and not has_mask:
                o_ref[t, rows, :] = jnp.full(shape, tbl_ref[h, buckets[0]] * scale, F32)
                continue
            idx = idx_ref[t, rows, :]
            acc = jnp.full(shape, NEG_INF, F32)
            for k in buckets:
                acc = jnp.where(idx == k, tbl_ref[h, k] * scale, acc)
            o_ref[t, rows, :] = acc


def _bias_tiles(table, idx, head_major, scale=1.0):
    t, r, c = idx.shape
    nh = table.shape[0]
    plan = tuple(
        tuple((tuple(int(k) for k in np.unique(blk[blk >= 0])), bool((blk < 0).any()))
              for blk in idx[i].reshape(r // BIAS_ROWS, BIAS_ROWS * c))
        for i in range(t))
    if head_major:
        out_shape, out_block, out_map = (nh, t, r, c), (None, t, r, c), (lambda h: (h, 0, 0, 0))
    else:
        out_shape, out_block, out_map = (t, nh, r, c), (t, None, r, c), (lambda h: (0, h, 0, 0))
    return pl.pallas_call(
        functools.partial(_bias_kernel, plan=plan, scale=scale),
        grid=(nh,),
        in_specs=[pl.BlockSpec(memory_space=pltpu.SMEM), pl.BlockSpec((t, r, c), lambda h: (0, 0, 0))],
        out_specs=pl.BlockSpec(out_block, out_map),
        out_shape=jax.ShapeDtypeStruct(out_shape, F32),
        compiler_params=_cparams(("arbitrary",)),
        name="rel_bias_tiles",
    )(table, jnp.asarray(idx))


def _dilated_bias_idx():
    qi = np.arange(ATTN_BLOCK)[:, None]
    ki = np.arange(2 * ATTN_BLOCK)[None, :]
    dist_u = qi + ATTN_BLOCK - ki
    out = []
    for window, dil in DIL_GROUPS:
        band = (dist_u >= 0) & (dist_u <= window // dil)
        out.append(np.where(band, _bucket_np(dist_u * dil), -1))
    return np.stack(out).astype(np.int32)


def _diff_bias_idx(seq, tile):
    nq = seq // tile
    i = np.arange(tile)[:, None]
    j = np.arange(tile)[None, :]
    tiles = []
    for d in range(nq):
        dist = d * tile + i - j
        tiles.append(np.where(dist >= 0, _bucket_np(dist), -1).astype(np.int32))
    nd = nq
    while nd > 1 and np.array_equal(tiles[nd - 1], tiles[nd - 2]):
        nd -= 1
    assert all(np.array_equal(tiles[d], tiles[nd - 1]) for d in range(nd - 1, nq))
    return np.stack(tiles[:nd])


def _rms(x, g, eps):
    return x * lax.rsqrt(jnp.mean(x * x, axis=-1, keepdims=True) + eps) * g


def _norm_matmul_kernel(x_ref, *refs, n_chunk):
    n_proj = len(refs) // 3
    x = x_ref[...]
    xs = x * lax.rsqrt(jnp.mean(x * x, axis=-1, keepdims=True) + RMS_EPS)
    for p in range(n_proj):
        g_ref, w_ref, o_ref = refs[2 * p], refs[2 * p + 1], refs[2 * n_proj + p]
        xn = (xs * g_ref[...]).astype(BF16)
        for j in range(o_ref.shape[-1] // n_chunk):
            sl = slice(j * n_chunk, (j + 1) * n_chunk)
            o_ref[:, sl] = jnp.dot(xn, w_ref[:, sl], preferred_element_type=F32).astype(o_ref.dtype)


def _norm_matmul(x2d, projs, tm=ROW_TILE, n_chunk=512):
    m, d = x2d.shape
    ins, in_specs = [], []
    for g, w in projs:
        ins += [g.reshape(1, d), w]
        in_specs += [pl.BlockSpec((1, d), lambda i: (0, 0)), pl.BlockSpec(w.shape, lambda i: (0, 0))]
    return pl.pallas_call(
        functools.partial(_norm_matmul_kernel, n_chunk=n_chunk),
        grid=(m // tm,),
        in_specs=[pl.BlockSpec((tm, d), lambda i: (i, 0))] + in_specs,
        out_specs=[pl.BlockSpec((tm, w.shape[1]), lambda i: (i, 0)) for _, w in projs],
        out_shape=[jax.ShapeDtypeStruct((m, w.shape[1]), BF16) for _, w in projs],
        compiler_params=_cparams(("arbitrary",)),
        name="norm_matmul",
    )(x2d, *ins)


def _proj_a_kernel(x_ref, g_ref, w_ref, *refs, n_chunk):
    out_refs, xn_sc = refs[:-1], refs[-1]
    tm = x_ref.shape[0]
    xn = _rms(x_ref[...], g_ref[...], RMS_EPS)
    nlb = xn.shape[1] // LANES
    for j in range(nlb):
        xn_sc[j] = xn[:, j * LANES:(j + 1) * LANES]
    gw = out_refs[0].shape[-1]
    for gi, (o_ref, (_, dil)) in enumerate(zip(out_refs, DIL_GROUPS)):
        per = tm // dil
        if dil == 1:
            xg = xn
        else:
            xg = jnp.concatenate(
                [jnp.concatenate([xn_sc[j, pl.ds(c, per, stride=dil), :] for c in range(dil)], axis=0)
                 for j in range(nlb)], axis=1)
        xg = xg.astype(BF16)
        for j in range(gw // n_chunk):
            res = jnp.dot(xg, w_ref[:, gi * gw + j * n_chunk:gi * gw + (j + 1) * n_chunk],
                          preferred_element_type=F32)
            if j == 0:
                res = res * (HEAD_DIM ** -0.5 * LOG2E)
            res = res.astype(o_ref.dtype)
            for c in range(dil):
                o_ref[c, :, j * n_chunk:(j + 1) * n_chunk] = res[c * per:(c + 1) * per, :]


def _proj_a(x2d, g, w_bf16, b, s, tm=ROW_TILE, n_chunk=512):
    m, d = x2d.shape
    gw = w_bf16.shape[1] // len(DIL_GROUPS)
    tps = s // tm
    assert all(tm % (16 * dil) == 0 for _, dil in DIL_GROUPS) and n_chunk == N_HEADS * HEAD_DIM
    return pl.pallas_call(
        functools.partial(_proj_a_kernel, n_chunk=n_chunk),
        grid=(b, tps),
        in_specs=[pl.BlockSpec((tm, d), lambda bb, i: (bb * tps + i, 0)),
                  pl.BlockSpec((1, d), lambda bb, i: (0, 0)),
                  pl.BlockSpec(w_bf16.shape, lambda bb, i: (0, 0))],
        out_specs=[pl.BlockSpec((None, dil, tm // dil, gw), lambda bb, i: (bb, 0, i, 0)) for _, dil in DIL_GROUPS],
        out_shape=[jax.ShapeDtypeStruct((b, dil, s // dil, gw), BF16) for _, dil in DIL_GROUPS],
        scratch_shapes=[pltpu.VMEM((d // LANES, tm, LANES), F32)],
        compiler_params=_cparams(("arbitrary", "arbitrary")),
        name="proj_dilated",
    )(x2d, g.reshape(1, d), w_bf16)


def _dil_kernel(q_ref, k_ref, v_ref, kp_ref, vp_ref, bias_ref, o_ref, lse_ref):
    blk = ATTN_BLOCK
    n_cls, tq = q_ref.shape[0], q_ref.shape[1]
    i = pl.program_id(2)
    lane = lax.broadcasted_iota(jnp.int32, (1, LANES), 1)
    col = lax.broadcasted_iota(jnp.int32, (blk, 2 * blk), 1)
    first_pen = jnp.where(jnp.logical_and(i == 0, col < blk), NEG_INF, 0.0).astype(F32)
    ones = jnp.ones((2 * blk, LANES), BF16)
    for cc in range(n_cls):
        for j in range(tq // blk):
            rows = slice(j * blk, (j + 1) * blk)
            qj = q_ref[cc, rows, :]
            if j == 0:
                kx = jnp.concatenate([kp_ref[cc], k_ref[cc, 0:blk, :]], axis=0)
                vx = jnp.concatenate([vp_ref[cc], v_ref[cc, 0:blk, :]], axis=0)
            else:
                kx = k_ref[cc, (j - 1) * blk:(j + 1) * blk, :]
                vx = v_ref[cc, (j - 1) * blk:(j + 1) * blk, :]
            lse_tile = jnp.zeros((blk, LANES), F32)
            for hp in range(N_HEADS // 2):
                cols = slice(hp * LANES, (hp + 1) * LANES)
                q2, k2 = qj[:, cols], kx[:, cols]
                v2 = jnp.concatenate([vx[:, cols], ones], axis=1)
                o_pair = None
                for e in range(2):
                    h = 2 * hp + e
                    sel = (lane >= HEAD_DIM) if e else (lane < HEAD_DIM)
                    qm = jnp.where(sel, q2, jnp.zeros_like(q2))
                    s = lax.dot_general(qm, k2, (((1,), (1,)), ((), ())), preferred_element_type=F32) + bias_ref[h]
                    if j == 0:
                        s = s + first_pen
                    m = jnp.max(s, axis=-1, keepdims=True)
                    p = jnp.exp2(s - m)
                    pv = jnp.dot(p.astype(BF16), v2, preferred_element_type=F32)
                    l = pv[:, LANES:]
                    o_e = pv[:, :LANES] * (1.0 / l)
                    o_pair = o_e if e == 0 else jnp.where(lane < HEAD_DIM, o_pair, o_e)
                    lse_tile = jnp.where(lane == h, m + jnp.log2(l), lse_tile)
                o_ref[cc, rows, cols] = o_pair.astype(o_ref.dtype)
            lse_ref[cc, rows, :] = lse_tile


def _dilated_group(qkv, bias, g, dil):
    b, r, n, _ = qkv.shape
    w = N_HEADS * HEAD_DIM
    assert r == dil and n % ATTN_BLOCK == 0
    tq = min(n, DIL_ROWS)
    n_cls = min(dil, DIL_ROWS // tq)
    sub = tq // ATTN_BLOCK

    def cur(which):
        return pl.BlockSpec((None, n_cls, tq, w), lambda bb, c, i: (bb, c, i, which))

    def prev(which):
        return pl.BlockSpec((None, n_cls, ATTN_BLOCK, w),
                            lambda bb, c, i: (bb, c, jnp.maximum(i * sub - 1, 0), which))

    return pl.pallas_call(
        _dil_kernel,
        grid=(b, dil // n_cls, n // tq),
        in_specs=[cur(0), cur(1), cur(2), prev(1), prev(2),
                  pl.BlockSpec((None, N_HEADS, ATTN_BLOCK, 2 * ATTN_BLOCK), lambda bb, c, i: (g, 0, 0, 0))],
        out_specs=[pl.BlockSpec((None, n_cls, tq, w), lambda bb, c, i: (bb, c, i, 0)),
                   pl.BlockSpec((None, n_cls, tq, LANES), lambda bb, c, i: (bb, c, i, 0))],
        out_shape=[jax.ShapeDtypeStruct((b, dil, n, w), BF16),
                   jax.ShapeDtypeStruct((b, dil, n, LANES), F32)],
        compiler_params=_cparams(("arbitrary", "arbitrary", "arbitrary")),
        name=f"dilated_attn_g{g}",
    )(qkv, qkv, qkv, qkv, qkv, bias)


def _combine_out_kernel(o0, o1, o2, l0, l1, l2, e_ref, w_ref, g_ref, h_ref, out_ref, o_sc, l_sc):
    tm = h_ref.shape[0]

    def token_order(o_ref, l_ref, slot):
        dil = o_ref.shape[0]
        if dil == 1:
            return o_ref[0].astype(F32), l_ref[0]
        per = tm // dil
        nlb = o_ref.shape[-1] // LANES
        for c in range(dil):
            oc = o_ref[c].astype(F32)
            for j in range(nlb):
                o_sc[slot, j, pl.ds(c, per, stride=dil), :] = oc[:, j * LANES:(j + 1) * LANES]
            l_sc[slot, pl.ds(c, per, stride=dil), :] = l_ref[c]
        return jnp.concatenate([o_sc[slot, j] for j in range(nlb)], axis=1), l_sc[slot]

    pairs = [token_order(o_ref, l_ref, slot) for slot, (o_ref, l_ref) in enumerate(((o0, l0), (o1, l1), (o2, l2)))]
    ls = [l for _, l in pairs]
    mx = jnp.maximum(jnp.maximum(ls[0], ls[1]), ls[2])
    ws = [jnp.exp2(l - mx) for l in ls]
    inv = 1.0 / (ws[0] + ws[1] + ws[2])
    a = None
    e = e_ref[...]
    for (og, _), wg in zip(pairs, ws):
        rem = wg * inv
        ax = None
        for _ in range(3):
            piece = rem.astype(BF16)
            rem = rem - piece.astype(F32)
            part = jnp.dot(piece, e, preferred_element_type=F32)
            ax = part if ax is None else ax + part
        t = ax * og
        a = t if a is None else a + t
    y = jnp.dot(a.astype(BF16), w_ref[...], preferred_element_type=F32)
    out_ref[...] = h_ref[...] + _rms(y, g_ref[...], RMS_EPS)


def _combine_out(os_, lses, w_bf16, g, h2d, b, s, tm=ROW_TILE):
    m, d = h2d.shape
    w = os_[0].shape[-1]
    tps = s // tm
    expand = np.zeros((LANES, w), np.float32)
    for hh in range(N_HEADS):
        expand[hh, hh * HEAD_DIM:(hh + 1) * HEAD_DIM] = 1.0
    grp = lambda a: pl.BlockSpec((None, a.shape[1], tm // a.shape[1], a.shape[3]), lambda bb, i: (bb, 0, i, 0))
    row = pl.BlockSpec((tm, d), lambda bb, i: (bb * tps + i, 0))
    full = lambda shp: pl.BlockSpec(shp, lambda bb, i: (0, 0))
    return pl.pallas_call(
        _combine_out_kernel,
        grid=(b, tps),
        in_specs=[grp(a) for a in os_] + [grp(a) for a in lses]
                 + [full((LANES, w)), full((w, d)), full((1, d)), row],
        out_specs=row,
        out_shape=jax.ShapeDtypeStruct((m, d), F32),
        scratch_shapes=[pltpu.VMEM((3, w // LANES, tm, LANES), F32), pltpu.VMEM((3, tm, LANES), F32)],
        compiler_params=_cparams(("arbitrary", "arbitrary")),
        name="combine_out_proj",
    )(*os_, *lses, jnp.asarray(expand, BF16), w_bf16, g.reshape(1, d), h2d)


def _ffn_kernel(*refs, tiles_per_seq, f_chunk, fuse_out_proj):
    if fuse_out_proj:
        a_ref, wo_ref, g_mix_ref, refs = refs[0], refs[1], refs[2], refs[3:]
    h_ref, g_in_ref, wup_ref, cw_ref, cb_ref, wdn_ref, g_out_ref, out_ref, h_sc, xn_sc, halo_sc, acc_sc = refs[:12]
    u_bufs = refs[12:]
    tm = h_ref.shape[0]
    d_ff = wdn_ref.shape[0]
    nc = d_ff // f_chunk
    i = pl.program_id(0)
    if fuse_out_proj:
        mix = jnp.dot(a_ref[...], wo_ref[...], preferred_element_type=F32)
        h_sc[...] = h_ref[...] + _rms(mix, g_mix_ref[...], RMS_EPS)
    else:
        h_sc[...] = h_ref[...]
    xn_sc[...] = _rms(h_sc[...], g_in_ref[...], RMS_EPS).astype(BF16)
    acc_sc[...] = jnp.zeros(acc_sc.shape, F32)
    seq_start = (i % tiles_per_seq) == 0
    inv_sqrt2 = 1.0 / math.sqrt(2.0)

    def cols(c, j):
        return slice(j * d_ff + c * f_chunk, j * d_ff + (c + 1) * f_chunk)

    def up(c, u_sc):
        for j in range(2):
            u_sc[j, SUBLANES:tm + SUBLANES, :] = jnp.dot(xn_sc[...], wup_ref[:, cols(c, j)],
                                                         preferred_element_type=F32)

    def conv(u_sc, c, j):
        cw = cw_ref[:, cols(c, j)]
        out = cb_ref[:, cols(c, j)]
        for t in range(CONV_WIDTH):
            off = SUBLANES - (CONV_WIDTH - 1 - t)
            out = out + cw[t:t + 1, :] * u_sc[j, off:tm + off, :]
        return out

    def down(c, u_sc):
        u_sc[:, 0:SUBLANES, :] = jnp.where(seq_start, 0.0, halo_sc[c])
        halo_sc[c] = u_sc[:, tm:tm + SUBLANES, :]
        gate = conv(u_sc, c, 0)
        val = conv(u_sc, c, 1)
        act = (0.5 * gate * (1.0 + lax.erf(gate * inv_sqrt2)) * val).astype(BF16)
        acc_sc[...] += jnp.dot(act, wdn_ref[c * f_chunk:(c + 1) * f_chunk, :], preferred_element_type=F32)

    for c in range(min(FFN_AHEAD, nc)):
        up(c, u_bufs[c])
    for c in range(nc):
        if c + FFN_AHEAD < nc:
            up(c + FFN_AHEAD, u_bufs[(c + FFN_AHEAD) % (FFN_AHEAD + 1)])
        down(c, u_bufs[c % (FFN_AHEAD + 1)])
    out_ref[...] = h_sc[...] + _rms(acc_sc[...], g_out_ref[...], RMS_EPS)


def _conv_ffn(h2d, g_in, w_up_bf16, conv_w, conv_b, w_down_bf16, g_out, seq, out_proj=None, tm=ROW_TILE, f_chunk=256):
    m, d = h2d.shape
    f2 = w_up_bf16.shape[1]
    d_ff = f2 // 2
    assert seq % tm == 0 and d_ff % f_chunk == 0 and CONV_WIDTH - 1 <= SUBLANES
    nc = d_ff // f_chunk
    full = lambda shp: pl.BlockSpec(shp, lambda i: (0, 0))
    row = lambda width: pl.BlockSpec((tm, width), lambda i: (i, 0))
    ins = [h2d, g_in.reshape(1, d), w_up_bf16, conv_w, conv_b.reshape(1, f2), w_down_bf16, g_out.reshape(1, d)]
    in_specs = [row(d), full((1, d)), full((d, f2)), full((CONV_WIDTH, f2)), full((1, f2)), full((d_ff, d)),
                full((1, d))]
    if out_proj is not None:
        a2d, w_out_bf16, g_mix = out_proj
        ins = [a2d, w_out_bf16, g_mix.reshape(1, d)] + ins
        in_specs = [row(a2d.shape[1]), full(w_out_bf16.shape), full((1, d))] + in_specs
    return pl.pallas_call(
        functools.partial(_ffn_kernel, tiles_per_seq=seq // tm, f_chunk=f_chunk, fuse_out_proj=out_proj is not None),
        grid=(m // tm,),
        in_specs=in_specs,
        out_specs=row(d),
        out_shape=jax.ShapeDtypeStruct((m, d), F32),
        scratch_shapes=[pltpu.VMEM((tm, d), F32), pltpu.VMEM((tm, d), BF16),
                        pltpu.VMEM((nc, 2, SUBLANES, f_chunk), F32), pltpu.VMEM((tm, d), F32)]
                       + [pltpu.VMEM((2, tm + SUBLANES, f_chunk), F32)] * (FFN_AHEAD + 1),
        compiler_params=_cparams(("arbitrary",)),
        name="conv_ffn",
    )(*ins)


def _diff_kernel(q_ref, k_ref, v_ref, bias_ref, lq1, lk1, lq2, lk2, sg_ref, o_ref,
                 qs_sc, s_sc, m_sc, l_sc, acc_sc, *, tile, n_bias, lambda_init, heads):
    qi = pl.program_id(2)
    hd2 = 2 * HEAD_DIM
    lane = lax.broadcasted_iota(jnp.int32, (1, LANES), 1)
    scale = jnp.asarray(HEAD_DIM ** -0.5, q_ref.dtype)
    for hh in range(heads):
        q = q_ref[:, hh * hd2:(hh + 1) * hd2]
        zero = jnp.zeros_like(q)
        qs_sc[hh, 0:tile, :] = jnp.where(lane < HEAD_DIM, q, zero) * scale
        qs_sc[hh, tile:2 * tile, :] = jnp.where(lane >= HEAD_DIM, q, zero) * scale
    m_sc[...] = jnp.full(m_sc.shape, NEG_INF, F32)
    l_sc[...] = jnp.zeros(l_sc.shape, F32)
    acc_sc[...] = jnp.zeros(acc_sc.shape, F32)
    ones = jnp.ones((tile, LANES), BF16)

    def scores(ki, hh, c):
        k = k_ref[pl.ds(pl.multiple_of(ki * tile, tile), tile), hh * hd2:(hh + 1) * hd2]
        return lax.dot_general(qs_sc[hh, c * tile:(c + 1) * tile, :], k, (((1,), (1,)), ((), ())),
                               preferred_element_type=F32)

    def update(hh, rows, s, vx):
        m_prev = m_sc[hh, rows, :]
        m_new = jnp.maximum(m_prev, jnp.max(s, axis=-1, keepdims=True))
        alpha = jnp.exp(m_prev - m_new)
        p = jnp.exp(s - jnp.concatenate([m_new] * (s.shape[1] // LANES), axis=1))
        pv = jnp.dot(p.astype(BF16), vx, preferred_element_type=F32)
        l_sc[hh, rows, :] = alpha * l_sc[hh, rows, :] + pv[:, hd2:]
        acc_sc[hh, rows, :] = alpha * acc_sc[hh, rows, :] + pv[:, :hd2]
        m_sc[hh, rows, :] = m_new

    def values(ki, hh):
        v = v_ref[pl.ds(pl.multiple_of(ki * tile, tile), tile), hh * hd2:(hh + 1) * hd2]
        return jnp.concatenate([v, ones], axis=1)

    def step(ki):
        d = jnp.minimum(qi - ki, n_bias - 1)
        for hh in range(heads):
            vx = values(ki, hh)
            for c in range(2):
                rows = slice(c * tile, (c + 1) * tile)
                s = s_sc[hh, rows, :] + bias_ref[hh, d]
                s_sc[hh, rows, :] = scores(ki + 1, hh, c)
                update(hh, rows, s, vx)

    def diagonal_step():
        half = tile // 2
        for hh in range(heads):
            vx = values(qi, hh)
            for c in range(2):
                for r0 in range(0, tile, half):
                    rows = slice(c * tile + r0, c * tile + r0 + half)
                    nk = r0 + half
                    s = s_sc[hh, rows, 0:nk] + bias_ref[hh, 0, r0:r0 + half, 0:nk]
                    update(hh, rows, s, vx[0:nk, :])

    for hh in range(heads):
        for c in range(2):
            s_sc[hh, c * tile:(c + 1) * tile, :] = scores(0, hh, c)

    def body(ki, carry):
        step(ki)
        return carry

    lax.fori_loop(0, qi, body, 0)
    diagonal_step()

    lam = (jnp.exp(jnp.sum(lq1[...].astype(F32) * lk1[...].astype(F32), axis=-1, keepdims=True))
           - jnp.exp(jnp.sum(lq2[...].astype(F32) * lk2[...].astype(F32), axis=-1, keepdims=True))
           + lambda_init)
    for hh in range(heads):
        o = acc_sc[hh] * (1.0 / l_sc[hh])
        a = o[0:tile, :] - lam * o[tile:2 * tile, :]
        y = _rms(a, sg_ref[...], SUBLN_EPS) * (1.0 - lambda_init)
        o_ref[:, hh * hd2:(hh + 1) * hd2] = y.astype(o_ref.dtype)


def _diff_attention(q, k, v, bias, lq1, lk1, lq2, lk2, subln_g, lambda_init, tile, heads=2):
    b, s, _ = q.shape
    n_bias = bias.shape[1]
    hd2 = 2 * HEAD_DIM
    hw = heads * hd2
    ng = N_HEADS // heads
    vec = lambda a: a.reshape(1, -1)
    small = lambda n: pl.BlockSpec((1, n), lambda h, bb, i: (0, 0))
    return pl.pallas_call(
        functools.partial(_diff_kernel, tile=tile, n_bias=n_bias, lambda_init=lambda_init, heads=heads),
        grid=(ng, b, s // tile),
        in_specs=[pl.BlockSpec((None, tile, hw), lambda h, bb, i: (bb, i, h)),
                  pl.BlockSpec((None, s, hw), lambda h, bb, i: (bb, 0, h)),
                  pl.BlockSpec((None, s, hw), lambda h, bb, i: (bb, 0, h)),
                  pl.BlockSpec((heads, n_bias, tile, tile), lambda h, bb, i: (h, 0, 0, 0)),
                  small(HEAD_DIM), small(HEAD_DIM), small(HEAD_DIM), small(HEAD_DIM), small(hd2)],
        out_specs=pl.BlockSpec((None, tile, hw), lambda h, bb, i: (bb, i, h)),
        out_shape=jax.ShapeDtypeStruct((b, s, N_HEADS * hd2), BF16),
        scratch_shapes=[pltpu.VMEM((heads, 2 * tile, hd2), BF16), pltpu.VMEM((heads, 2 * tile, tile), F32),
                        pltpu.VMEM((heads, 2 * tile, LANES), F32), pltpu.VMEM((heads, 2 * tile, LANES), F32),
                        pltpu.VMEM((heads, 2 * tile, hd2), F32)],
        compiler_params=_cparams(("arbitrary", "arbitrary", "arbitrary")),
        name="diff_attn",
    )(q, k, v, bias, vec(lq1), vec(lk1), vec(lq2), vec(lk2), vec(subln_g))


DIFF_TILE = 512
DIL_ROWS = 1024
FFN_AHEAD = 6


def kernel(x, rel_bias_table, norm_g, w_in_a, w_out_a, kv_norm_g, w_k_shared, w_v_shared, w_q_b,
           lam_q1, lam_k1, lam_q2, lam_k2, subln_g, w_out_b, w_up, conv_w, conv_b, w_down):
    b, s, d = x.shape
    depth = norm_g.shape[0]
    n_a = w_in_a.shape[0]
    bf = lambda a: a.astype(BF16)

    dil_bias = _bias_tiles(rel_bias_table, _dilated_bias_idx(), head_major=False, scale=LOG2E)
    diff_tile = min(DIFF_TILE, s)
    diff_bias = _bias_tiles(rel_bias_table, _diff_bias_idx(s, diff_tile), head_major=True)

    h = x.reshape(b * s, d)
    k_sh = v_sh = q = None
    for layer in range(depth):
        g = norm_g[layer]
        if layer < n_a:
            qkvs = _proj_a(h, g[0], bf(w_in_a[layer]), b, s)
            outs = [_dilated_group(qkvs[gi], dil_bias, gi, dil) for gi, (_, dil) in enumerate(DIL_GROUPS)]
            h = _combine_out([o for o, _ in outs], [l for _, l in outs], bf(w_out_a[layer]), g[1], h, b, s)
            mixer_out = None
        else:
            j = layer - n_a
            lambda_init = 0.8 - 0.6 * math.exp(-0.3 * layer)
            if q is None:
                q, = _norm_matmul(h, [(g[0], bf(w_q_b[j]))])
            a = _diff_attention(q.reshape(b, s, -1), k_sh, v_sh, diff_bias, lam_q1[j], lam_k1[j], lam_q2[j],
                                lam_k2[j], subln_g[j], lambda_init, diff_tile)
            q = None
            mixer_out = (a.reshape(b * s, -1), bf(w_out_b[j]), g[1])
        h = _conv_ffn(h, g[2], bf(w_up[layer]), conv_w[layer], conv_b[layer], bf(w_down[layer]), g[3], s,
                      out_proj=mixer_out)
        if layer == n_a - 1:
            projs = [(kv_norm_g, bf(w_k_shared)), (kv_norm_g, bf(w_v_shared))]
            if layer + 1 < depth:
                projs.append((norm_g[layer + 1, 0], bf(w_q_b[0])))
            res = _norm_matmul(h, projs)
            k_sh, v_sh = res[0].reshape(b, s, -1), res[1].reshape(b, s, -1)
            q = res[2] if len(res) > 2 else None
    return h.reshape(b, s, d)
```

```python
import functools
import math

import numpy as np
import jax
import jax.numpy as jnp
from jax import lax
from jax.experimental import pallas as pl
from jax.experimental.pallas import tpu as pltpu

F32 = jnp.float32
BF16 = jnp.bfloat16

HEAD_DIM = 64
N_HEADS = 8
DIL_GROUPS = ((128, 1), (512, 4), (2048, 16))
ATTN_BLOCK = 128
NUM_BUCKETS = 32
MAX_DISTANCE = 2048
CONV_WIDTH = 3
RMS_EPS = 1e-6
SUBLN_EPS = 1e-5

LANES = 128
SUBLANES = 8
ROW_TILE = 512
BIAS_ROWS = 32
VMEM_LIMIT = 56 * 1024 * 1024

NEG_INF = float("-inf")
LOG2E = math.log2(math.e)
Q_SCALE = HEAD_DIM ** -0.5 * LOG2E


def _cparams(sem):
    return pltpu.CompilerParams(dimension_semantics=sem, vmem_limit_bytes=VMEM_LIMIT)


def _bucket_np(dist):
    n = np.maximum(dist, 0)
    max_exact = NUM_BUCKETS // 2

    def large(dtype):
        nf = np.maximum(n, 1).astype(dtype)
        v = np.log(nf / dtype(max_exact)) / dtype(math.log(MAX_DISTANCE / max_exact)) * dtype(NUM_BUCKETS - max_exact)
        return np.minimum(max_exact + v.astype(np.int32), NUM_BUCKETS - 1)

    l32, l64 = large(np.float32), large(np.float64)
    assert np.array_equal(l32, l64), "bucket boundary is rounding sensitive"
    return np.where(n < max_exact, n, l32).astype(np.int32)


def _bias_kernel(tbl_ref, idx_ref, o_ref, *, plan, scale):
    h = pl.program_id(0)
    for t, blocks in enumerate(plan):
        for rb, (buckets, has_mask) in enumerate(blocks):
            rows = slice(rb * BIAS_ROWS, (rb + 1) * BIAS_ROWS)
            shape = (BIAS_ROWS, idx_ref.shape[-1])
            if len(buckets) == 1 and not has_mask:
                o_ref[t, rows, :] = jnp.full(shape, tbl_ref[h, buckets[0]] * scale, F32)
                continue
            idx = idx_ref[t, rows, :]
            acc = jnp.full(shape, NEG_INF, F32)
            for k in buckets:
                acc = jnp.where(idx == k, tbl_ref[h, k] * scale, acc)
            o_ref[t, rows, :] = acc


def _bias_tiles(table, idx, head_major, scale=1.0):
    t, r, c = idx.shape
    nh = table.shape[0]
    plan = tuple(
        tuple((tuple(int(k) for k in np.unique(blk[blk >= 0])), bool((blk < 0).any()))
              for blk in idx[i].reshape(r // BIAS_ROWS, BIAS_ROWS * c))
        for i in range(t))
    if head_major:
        out_shape, out_block, out_map = (nh, t, r, c), (None, t, r, c), (lambda h: (h, 0, 0, 0))
    else:
        out_shape, out_block, out_map = (t, nh, r, c), (t, None, r, c), (lambda h: (0, h, 0, 0))
    return pl.pallas_call(
        functools.partial(_bias_kernel, plan=plan, scale=scale),
        grid=(nh,),
        in_specs=[pl.BlockSpec(memory_space=pltpu.SMEM), pl.BlockSpec((t, r, c), lambda h: (0, 0, 0))],
        out_specs=pl.BlockSpec(out_block, out_map),
        out_shape=jax.ShapeDtypeStruct(out_shape, F32),
        compiler_params=_cparams(("arbitrary",)),
        name="rel_bias_tiles",
    )(table, jnp.asarray(idx))


def _dilated_bias_idx():
    qi = np.arange(ATTN_BLOCK)[:, None]
    ki = np.arange(2 * ATTN_BLOCK)[None, :]
    dist_u = qi + ATTN_BLOCK - ki
    out = []
    for window, dil in DIL_GROUPS:
        band = (dist_u >= 0) & (dist_u <= window // dil)
        out.append(np.where(band, _bucket_np(dist_u * dil), -1))
    return np.stack(out).astype(np.int32)


def _diff_bias_idx(seq, tile):
    nq = seq // tile
    i = np.arange(tile)[:, None]
    j = np.arange(tile)[None, :]
    tiles = []
    for d in range(nq):
        dist = d * tile + i - j
        tiles.append(np.where(dist >= 0, _bucket_np(dist), -1).astype(np.int32))
    nd = nq
    while nd > 1 and np.array_equal(tiles[nd - 1], tiles[nd - 2]):
        nd -= 1
    assert all(np.array_equal(tiles[d], tiles[nd - 1]) for d in range(nd - 1, nq))
    return np.stack(tiles[:nd])


def _rms(x, g, eps):
    return x * lax.rsqrt(jnp.mean(x * x, axis=-1, keepdims=True) + eps) * g


def _norm_matmul_kernel(x_ref, *refs, n_chunk, out_scales):
    n_proj = len(out_scales)
    x = x_ref[...]
    xs = x * lax.rsqrt(jnp.mean(x * x, axis=-1, keepdims=True) + RMS_EPS)
    for p in range(n_proj):
        g_ref, w_ref, o_ref = refs[2 * p], refs[2 * p + 1], refs[2 * n_proj + p]
        xn = (xs * g_ref[...]).astype(BF16)
        for j in range(o_ref.shape[-1] // n_chunk):
            sl = slice(j * n_chunk, (j + 1) * n_chunk)
            res = jnp.dot(xn, w_ref[:, sl], preferred_element_type=F32)
            if out_scales[p] != 1.0:
                res = res * out_scales[p]
            o_ref[:, sl] = res.astype(o_ref.dtype)


def _norm_matmul(x2d, projs, tm=ROW_TILE, n_chunk=512):
    m, d = x2d.shape
    ins, in_specs = [], []
    out_scales = tuple(float(sc) for _, _, sc in projs)
    projs = [(g, w) for g, w, _ in projs]
    for g, w in projs:
        ins += [g.reshape(1, d), w]
        in_specs += [pl.BlockSpec((1, d), lambda i: (0, 0)), pl.BlockSpec(w.shape, lambda i: (0, 0))]
    return pl.pallas_call(
        functools.partial(_norm_matmul_kernel, n_chunk=n_chunk, out_scales=out_scales),
        grid=(m // tm,),
        in_specs=[pl.BlockSpec((tm, d), lambda i: (i, 0))] + in_specs,
        out_specs=[pl.BlockSpec((tm, w.shape[1]), lambda i: (i, 0)) for _, w in projs],
        out_shape=[jax.ShapeDtypeStruct((m, w.shape[1]), BF16) for _, w in projs],
        compiler_params=_cparams(("arbitrary",)),
        name="norm_matmul",
    )(x2d, *ins)


def _proj_a_kernel(x_ref, g_ref, w_ref, *refs, n_chunk):
    out_refs, xn_sc = refs[:-1], refs[-1]
    tm = x_ref.shape[0]
    xn = _rms(x_ref[...], g_ref[...], RMS_EPS)
    nlb = xn.shape[1] // LANES
    for j in range(nlb):
        xn_sc[j] = xn[:, j * LANES:(j + 1) * LANES]
    gw = out_refs[0].shape[-1]
    for gi, (o_ref, (_, dil)) in enumerate(zip(out_refs, DIL_GROUPS)):
        per = tm // dil
        if dil == 1:
            xg = xn
        else:
            xg = jnp.concatenate(
                [jnp.concatenate([xn_sc[j, pl.ds(c, per, stride=dil), :] for c in range(dil)], axis=0)
                 for j in range(nlb)], axis=1)
        xg = xg.astype(BF16)
        for j in range(gw // n_chunk):
            res = jnp.dot(xg, w_ref[:, gi * gw + j * n_chunk:gi * gw + (j + 1) * n_chunk],
                          preferred_element_type=F32)
            if j == 0:
                res = res * Q_SCALE
            res = res.astype(o_ref.dtype)
            for c in range(dil):
                o_ref[c, :, j * n_chunk:(j + 1) * n_chunk] = res[c * per:(c + 1) * per, :]


def _proj_a(x2d, g, w_bf16, b, s, tm=ROW_TILE, n_chunk=512):
    m, d = x2d.shape
    gw = w_bf16.shape[1] // len(DIL_GROUPS)
    tps = s // tm
    assert all(tm % (16 * dil) == 0 for _, dil in DIL_GROUPS) and n_chunk == N_HEADS * HEAD_DIM
    return pl.pallas_call(
        functools.partial(_proj_a_kernel, n_chunk=n_chunk),
        grid=(b, tps),
        in_specs=[pl.BlockSpec((tm, d), lambda bb, i: (bb * tps + i, 0)),
                  pl.BlockSpec((1, d), lambda bb, i: (0, 0)),
                  pl.BlockSpec(w_bf16.shape, lambda bb, i: (0, 0))],
        out_specs=[pl.BlockSpec((None, dil, tm // dil, gw), lambda bb, i: (bb, 0, i, 0)) for _, dil in DIL_GROUPS],
        out_shape=[jax.ShapeDtypeStruct((b, dil, s // dil, gw), BF16) for _, dil in DIL_GROUPS],
        scratch_shapes=[pltpu.VMEM((d // LANES, tm, LANES), F32)],
        compiler_params=_cparams(("arbitrary", "arbitrary")),
        name="proj_dilated",
    )(x2d, g.reshape(1, d), w_bf16)


def _dil_kernel(q_ref, k_ref, v_ref, kp_ref, vp_ref, bias_ref, o_ref, lse_ref):
    blk = ATTN_BLOCK
    n_cls, tq = q_ref.shape[0], q_ref.shape[1]
    i = pl.program_id(2)
    lane = lax.broadcasted_iota(jnp.int32, (1, LANES), 1)
    col = lax.broadcasted_iota(jnp.int32, (blk, 2 * blk), 1)
    first_pen = jnp.where(jnp.logical_and(i == 0, col < blk), NEG_INF, 0.0).astype(F32)
    ones = jnp.ones((2 * blk, LANES), BF16)
    for cc in range(n_cls):
        for j in range(tq // blk):
            rows = slice(j * blk, (j + 1) * blk)
            qj = q_ref[cc, rows, :]
            if j == 0:
                kx = jnp.concatenate([kp_ref[cc], k_ref[cc, 0:blk, :]], axis=0)
                vx = jnp.concatenate([vp_ref[cc], v_ref[cc, 0:blk, :]], axis=0)
            else:
                kx = k_ref[cc, (j - 1) * blk:(j + 1) * blk, :]
                vx = v_ref[cc, (j - 1) * blk:(j + 1) * blk, :]
            lse_tile = jnp.zeros((blk, LANES), F32)
            for hp in range(N_HEADS // 2):
                cols = slice(hp * LANES, (hp + 1) * LANES)
                q2, k2 = qj[:, cols], kx[:, cols]
                v2 = jnp.concatenate([vx[:, cols], ones], axis=1)
                o_pair = None
                for e in range(2):
                    h = 2 * hp + e
                    sel = (lane >= HEAD_DIM) if e else (lane < HEAD_DIM)
                    qm = jnp.where(sel, q2, jnp.zeros_like(q2))
                    s = lax.dot_general(qm, k2, (((1,), (1,)), ((), ())), preferred_element_type=F32) + bias_ref[h]
                    if j == 0:
                        s = s + first_pen
                    m = jnp.max(s, axis=-1, keepdims=True)
                    p = jnp.exp2(s - m)
                    pv = jnp.dot(p.astype(BF16), v2, preferred_element_type=F32)
                    l = pv[:, LANES:]
                    o_e = pv[:, :LANES] * (1.0 / l)
                    o_pair = o_e if e == 0 else jnp.where(lane < HEAD_DIM, o_pair, o_e)
                    lse_tile = jnp.where(lane == h, m + jnp.log2(l), lse_tile)
                o_ref[cc, rows, cols] = o_pair.astype(o_ref.dtype)
            lse_ref[cc, rows, :] = lse_tile


def _dilated_group(qkv, bias, g, dil):
    b, r, n, _ = qkv.shape
    w = N_HEADS * HEAD_DIM
    assert r == dil and n % ATTN_BLOCK == 0
    tq = min(n, DIL_ROWS)
    n_cls = min(dil, DIL_ROWS // tq)
    sub = tq // ATTN_BLOCK

    def cur(which):
        return pl.BlockSpec((None, n_cls, tq, w), lambda bb, c, i: (bb, c, i, which))

    def prev(which):
        return pl.BlockSpec((None, n_cls, ATTN_BLOCK, w),
                            lambda bb, c, i: (bb, c, jnp.maximum(i * sub - 1, 0), which))

    return pl.pallas_call(
        _dil_kernel,
        grid=(b, dil // n_cls, n // tq),
        in_specs=[cur(0), cur(1), cur(2), prev(1), prev(2),
                  pl.BlockSpec((None, N_HEADS, ATTN_BLOCK, 2 * ATTN_BLOCK), lambda bb, c, i: (g, 0, 0, 0))],
        out_specs=[pl.BlockSpec((None, n_cls, tq, w), lambda bb, c, i: (bb, c, i, 0)),
                   pl.BlockSpec((None, n_cls, tq, LANES), lambda bb, c, i: (bb, c, i, 0))],
        out_shape=[jax.ShapeDtypeStruct((b, dil, n, w), BF16),
                   jax.ShapeDtypeStruct((b, dil, n, LANES), F32)],
        compiler_params=_cparams(("arbitrary", "arbitrary", "arbitrary")),
        name=f"dilated_attn_g{g}",
    )(qkv, qkv, qkv, qkv, qkv, bias)


def _combine_out_kernel(o0, o1, o2, l0, l1, l2, e_ref, w_ref, g_ref, h_ref, out_ref, o_sc, l_sc):
    tm = h_ref.shape[0]

    def token_order(o_ref, l_ref, slot):
        dil = o_ref.shape[0]
        if dil == 1:
            return o_ref[0].astype(F32), l_ref[0]
        per = tm // dil
        nlb = o_ref.shape[-1] // LANES
        for c in range(dil):
            oc = o_ref[c].astype(F32)
            for j in range(nlb):
                o_sc[slot, j, pl.ds(c, per, stride=dil), :] = oc[:, j * LANES:(j + 1) * LANES]
            l_sc[slot, pl.ds(c, per, stride=dil), :] = l_ref[c]
        return jnp.concatenate([o_sc[slot, j] for j in range(nlb)], axis=1), l_sc[slot]

    pairs = [token_order(o_ref, l_ref, slot) for slot, (o_ref, l_ref) in enumerate(((o0, l0), (o1, l1), (o2, l2)))]
    ls = [l for _, l in pairs]
    mx = jnp.maximum(jnp.maximum(ls[0], ls[1]), ls[2])
    ws = [jnp.exp2(l - mx) for l in ls]
    inv = 1.0 / (ws[0] + ws[1] + ws[2])
    e = e_ref[...]

    def expand(wg):
        rem = wg * inv
        ax = None
        for _ in range(3):
            piece = rem.astype(BF16)
            rem = rem - piece.astype(F32)
            part = jnp.dot(piece, e, preferred_element_type=F32)
            ax = part if ax is None else ax + part
        return ax

    o_last = pairs[-1][0]
    a = o_last
    for (og, _), wg in zip(pairs[:-1], ws[:-1]):
        a = a + expand(wg) * (og - o_last)
    y = jnp.dot(a.astype(BF16), w_ref[...], preferred_element_type=F32)
    out_ref[...] = h_ref[...] + _rms(y, g_ref[...], RMS_EPS)


def _combine_out(os_, lses, w_bf16, g, h2d, b, s, tm=ROW_TILE):
    m, d = h2d.shape
    w = os_[0].shape[-1]
    tps = s // tm
    expand = np.zeros((LANES, w), np.float32)
    for hh in range(N_HEADS):
        expand[hh, hh * HEAD_DIM:(hh + 1) * HEAD_DIM] = 1.0
    grp = lambda a: pl.BlockSpec((None, a.shape[1], tm // a.shape[1], a.shape[3]), lambda bb, i: (bb, 0, i, 0))
    row = pl.BlockSpec((tm, d), lambda bb, i: (bb * tps + i, 0))
    full = lambda shp: pl.BlockSpec(shp, lambda bb, i: (0, 0))
    return pl.pallas_call(
        _combine_out_kernel,
        grid=(b, tps),
        in_specs=[grp(a) for a in os_] + [grp(a) for a in lses]
                 + [full((LANES, w)), full((w, d)), full((1, d)), row],
        out_specs=row,
        out_shape=jax.ShapeDtypeStruct((m, d), F32),
        scratch_shapes=[pltpu.VMEM((3, w // LANES, tm, LANES), F32), pltpu.VMEM((3, tm, LANES), F32)],
        compiler_params=_cparams(("arbitrary", "arbitrary")),
        name="combine_out_proj",
    )(*os_, *lses, jnp.asarray(expand, BF16), w_bf16, g.reshape(1, d), h2d)


def _ffn_kernel(*refs, tiles_per_seq, f_chunk, fuse_out_proj):
    if fuse_out_proj:
        a_ref, wo_ref, g_mix_ref, refs = refs[0], refs[1], refs[2], refs[3:]
    h_ref, g_in_ref, wup_ref, cw_ref, cb_ref, wdn_ref, g_out_ref, out_ref, h_sc, xn_sc, halo_sc, acc_sc = refs[:12]
    u_bufs = refs[12:]
    tm = h_ref.shape[0]
    d_ff = wdn_ref.shape[0]
    nc = d_ff // f_chunk
    i = pl.program_id(0)
    if fuse_out_proj:
        mix = jnp.dot(a_ref[...], wo_ref[...], preferred_element_type=F32)
        h_sc[...] = h_ref[...] + _rms(mix, g_mix_ref[...], RMS_EPS)
    else:
        h_sc[...] = h_ref[...]
    xn_sc[...] = _rms(h_sc[...], g_in_ref[...], RMS_EPS).astype(BF16)
    acc_sc[...] = jnp.zeros(acc_sc.shape, F32)
    seq_start = (i % tiles_per_seq) == 0
    inv_sqrt2 = 1.0 / math.sqrt(2.0)

    def cols(c, j):
        return slice(j * d_ff + c * f_chunk, j * d_ff + (c + 1) * f_chunk)

    def up(c, u_sc):
        for j in range(2):
            u_sc[j, SUBLANES:tm + SUBLANES, :] = jnp.dot(xn_sc[...], wup_ref[:, cols(c, j)],
                                                         preferred_element_type=F32)

    def conv(u_sc, c, j):
        cw = cw_ref[:, cols(c, j)]
        out = cb_ref[:, cols(c, j)]
        for t in range(CONV_WIDTH):
            off = SUBLANES - (CONV_WIDTH - 1 - t)
            out = out + cw[t:t + 1, :] * u_sc[j, off:tm + off, :]
        return out

    def down(c, u_sc):
        u_sc[:, 0:SUBLANES, :] = jnp.where(seq_start, 0.0, halo_sc[c])
        halo_sc[c] = u_sc[:, tm:tm + SUBLANES, :]
        gate = conv(u_sc, c, 0)
        val = conv(u_sc, c, 1)
        act = (0.5 * gate * (1.0 + lax.erf(gate * inv_sqrt2)) * val).astype(BF16)
        acc_sc[...] += jnp.dot(act, wdn_ref[c * f_chunk:(c + 1) * f_chunk, :], preferred_element_type=F32)

    for c in range(min(FFN_AHEAD, nc)):
        up(c, u_bufs[c])
    for c in range(nc):
        if c + FFN_AHEAD < nc:
            up(c + FFN_AHEAD, u_bufs[(c + FFN_AHEAD) % (FFN_AHEAD + 1)])
        down(c, u_bufs[c % (FFN_AHEAD + 1)])
    out_ref[...] = h_sc[...] + _rms(acc_sc[...], g_out_ref[...], RMS_EPS)


def _conv_ffn(h2d, g_in, w_up_bf16, conv_w, conv_b, w_down_bf16, g_out, seq, out_proj=None, tm=ROW_TILE, f_chunk=256):
    m, d = h2d.shape
    f2 = w_up_bf16.shape[1]
    d_ff = f2 // 2
    assert seq % tm == 0 and d_ff % f_chunk == 0 and CONV_WIDTH - 1 <= SUBLANES
    nc = d_ff // f_chunk
    full = lambda shp: pl.BlockSpec(shp, lambda i: (0, 0))
    row = lambda width: pl.BlockSpec((tm, width), lambda i: (i, 0))
    ins = [h2d, g_in.reshape(1, d), w_up_bf16, conv_w, conv_b.reshape(1, f2), w_down_bf16, g_out.reshape(1, d)]
    in_specs = [row(d), full((1, d)), full((d, f2)), full((CONV_WIDTH, f2)), full((1, f2)), full((d_ff, d)),
                full((1, d))]
    if out_proj is not None:
        a2d, w_out_bf16, g_mix = out_proj
        ins = [a2d, w_out_bf16, g_mix.reshape(1, d)] + ins
        in_specs = [row(a2d.shape[1]), full(w_out_bf16.shape), full((1, d))] + in_specs
    return pl.pallas_call(
        functools.partial(_ffn_kernel, tiles_per_seq=seq // tm, f_chunk=f_chunk, fuse_out_proj=out_proj is not None),
        grid=(m // tm,),
        in_specs=in_specs,
        out_specs=row(d),
        out_shape=jax.ShapeDtypeStruct((m, d), F32),
        scratch_shapes=[pltpu.VMEM((tm, d), F32), pltpu.VMEM((tm, d), BF16),
                        pltpu.VMEM((nc, 2, SUBLANES, f_chunk), F32), pltpu.VMEM((tm, d), F32)]
                       + [pltpu.VMEM((2, tm + SUBLANES, f_chunk), F32)] * (FFN_AHEAD + 1),
        compiler_params=_cparams(("arbitrary",)),
        name="conv_ffn",
    )(*ins)


def _diff_kernel(q_ref, k_ref, v_ref, bias_ref, lq1, lk1, lq2, lk2, sg_ref, o_ref,
                 qs_sc, s_sc, m_sc, l_sc, acc_sc, *, tile, n_bias, lambda_init, heads):
    qi = pl.program_id(2)
    hd2 = 2 * HEAD_DIM
    lane = lax.broadcasted_iota(jnp.int32, (1, LANES), 1)
    for hh in range(heads):
        q = q_ref[:, hh * hd2:(hh + 1) * hd2]
        zero = jnp.zeros_like(q)
        qs_sc[hh, 0:tile, :] = jnp.where(lane < HEAD_DIM, q, zero)
        qs_sc[hh, tile:2 * tile, :] = jnp.where(lane >= HEAD_DIM, q, zero)
    m_sc[...] = jnp.full(m_sc.shape, NEG_INF, F32)
    l_sc[...] = jnp.zeros(l_sc.shape, F32)
    acc_sc[...] = jnp.zeros(acc_sc.shape, F32)
    ones = jnp.ones((tile, LANES), BF16)

    def scores(ki, hh, c):
        k = k_ref[pl.ds(pl.multiple_of(ki * tile, tile), tile), hh * hd2:(hh + 1) * hd2]
        return lax.dot_general(qs_sc[hh, c * tile:(c + 1) * tile, :], k, (((1,), (1,)), ((), ())),
                               preferred_element_type=F32)

    def update(hh, rows, s, vx):
        m_prev = m_sc[hh, rows, :]
        m_new = jnp.maximum(m_prev, jnp.max(s, axis=-1, keepdims=True))
        alpha = jnp.exp2(m_prev - m_new)
        p = jnp.exp2(s - jnp.concatenate([m_new] * (s.shape[1] // LANES), axis=1))
        pv = jnp.dot(p.astype(BF16), vx, preferred_element_type=F32)
        l_sc[hh, rows, :] = alpha * l_sc[hh, rows, :] + pv[:, hd2:]
        acc_sc[hh, rows, :] = alpha * acc_sc[hh, rows, :] + pv[:, :hd2]
        m_sc[hh, rows, :] = m_new

    def values(ki, hh):
        v = v_ref[pl.ds(pl.multiple_of(ki * tile, tile), tile), hh * hd2:(hh + 1) * hd2]
        return jnp.concatenate([v, ones], axis=1)

    def step(ki):
        d = jnp.minimum(qi - ki, n_bias - 1)
        for hh in range(heads):
            vx = values(ki, hh)
            for c in range(2):
                rows = slice(c * tile, (c + 1) * tile)
                s = s_sc[hh, rows, :] + bias_ref[hh, d]
                s_sc[hh, rows, :] = scores(ki + 1, hh, c)
                update(hh, rows, s, vx)

    def diagonal_step():
        half = tile // 2
        for hh in range(heads):
            vx = values(qi, hh)
            for c in range(2):
                for r0 in range(0, tile, half):
                    rows = slice(c * tile + r0, c * tile + r0 + half)
                    nk = r0 + half
                    s = s_sc[hh, rows, 0:nk] + bias_ref[hh, 0, r0:r0 + half, 0:nk]
                    update(hh, rows, s, vx[0:nk, :])

    for hh in range(heads):
        for c in range(2):
            s_sc[hh, c * tile:(c + 1) * tile, :] = scores(0, hh, c)

    def body(ki, carry):
        step(ki)
        return carry

    lax.fori_loop(0, qi, body, 0)
    diagonal_step()

    lam = (jnp.exp(jnp.sum(lq1[...].astype(F32) * lk1[...].astype(F32), axis=-1, keepdims=True))
           - jnp.exp(jnp.sum(lq2[...].astype(F32) * lk2[...].astype(F32), axis=-1, keepdims=True))
           + lambda_init)
    for hh in range(heads):
        o = acc_sc[hh] * (1.0 / l_sc[hh])
        a = o[0:tile, :] - lam * o[tile:2 * tile, :]
        y = _rms(a, sg_ref[...], SUBLN_EPS) * (1.0 - lambda_init)
        o_ref[:, hh * hd2:(hh + 1) * hd2] = y.astype(o_ref.dtype)


def _diff_attention(q, k, v, bias, lq1, lk1, lq2, lk2, subln_g, lambda_init, tile, heads=2):
    b, s, _ = q.shape
    n_bias = bias.shape[1]
    hd2 = 2 * HEAD_DIM
    hw = heads * hd2
    ng = N_HEADS // heads
    vec = lambda a: a.reshape(1, -1)
    small = lambda n: pl.BlockSpec((1, n), lambda h, bb, i: (0, 0))
    return pl.pallas_call(
        functools.partial(_diff_kernel, tile=tile, n_bias=n_bias, lambda_init=lambda_init, heads=heads),
        grid=(ng, b, s // tile),
        in_specs=[pl.BlockSpec((None, tile, hw), lambda h, bb, i: (bb, i, h)),
                  pl.BlockSpec((None, s, hw), lambda h, bb, i: (bb, 0, h)),
                  pl.BlockSpec((None, s, hw), lambda h, bb, i: (bb, 0, h)),
                  pl.BlockSpec((heads, n_bias, tile, tile), lambda h, bb, i: (h, 0, 0, 0)),
                  small(HEAD_DIM), small(HEAD_DIM), small(HEAD_DIM), small(HEAD_DIM), small(hd2)],
        out_specs=pl.BlockSpec((None, tile, hw), lambda h, bb, i: (bb, i, h)),
        out_shape=jax.ShapeDtypeStruct((b, s, N_HEADS * hd2), BF16),
        scratch_shapes=[pltpu.VMEM((heads, 2 * tile, hd2), BF16), pltpu.VMEM((heads, 2 * tile, tile), F32),
                        pltpu.VMEM((heads, 2 * tile, LANES), F32), pltpu.VMEM((heads, 2 * tile, LANES), F32),
                        pltpu.VMEM((heads, 2 * tile, hd2), F32)],
        compiler_params=_cparams(("arbitrary", "arbitrary", "arbitrary")),
        name="diff_attn",
    )(q, k, v, bias, vec(lq1), vec(lk1), vec(lq2), vec(lk2), vec(subln_g))


DIFF_TILE = 512
DIL_ROWS = 1024
FFN_AHEAD = 6


def kernel(x, rel_bias_table, norm_g, w_in_a, w_out_a, kv_norm_g, w_k_shared, w_v_shared, w_q_b,
           lam_q1, lam_k1, lam_q2, lam_k2, subln_g, w_out_b, w_up, conv_w, conv_b, w_down):
    b, s, d = x.shape
    depth = norm_g.shape[0]
    n_a = w_in_a.shape[0]
    bf = lambda a: a.astype(BF16)

    dil_bias = _bias_tiles(rel_bias_table, _dilated_bias_idx(), head_major=False, scale=LOG2E)
    diff_tile = min(DIFF_TILE, s)
    diff_bias = _bias_tiles(rel_bias_table, _diff_bias_idx(s, diff_tile), head_major=True, scale=LOG2E)

    h = x.reshape(b * s, d)
    k_sh = v_sh = q = None
    for layer in range(depth):
        g = norm_g[layer]
        if layer < n_a:
            qkvs = _proj_a(h, g[0], bf(w_in_a[layer]), b, s)
            outs = [_dilated_group(qkvs[gi], dil_bias, gi, dil) for gi, (_, dil) in enumerate(DIL_GROUPS)]
            h = _combine_out([o for o, _ in outs], [l for _, l in outs], bf(w_out_a[layer]), g[1], h, b, s)
            mixer_out = None
        else:
            j = layer - n_a
            lambda_init = 0.8 - 0.6 * math.exp(-0.3 * layer)
            if q is None:
                q, = _norm_matmul(h, [(g[0], bf(w_q_b[j]), Q_SCALE)])
            a = _diff_attention(q.reshape(b, s, -1), k_sh, v_sh, diff_bias, lam_q1[j], lam_k1[j], lam_q2[j],
                                lam_k2[j], subln_g[j], lambda_init, diff_tile)
            q = None
            mixer_out = (a.reshape(b * s, -1), bf(w_out_b[j]), g[1])
        h = _conv_ffn(h, g[2], bf(w_up[layer]), conv_w[layer], conv_b[layer], bf(w_down[layer]), g[3], s,
                      out_proj=mixer_out)
        if layer == n_a - 1:
            projs = [(kv_norm_g, bf(w_k_shared), 1.0), (kv_norm_g, bf(w_v_shared), 1.0)]
            if layer + 1 < depth:
                projs.append((norm_g[layer + 1, 0], bf(w_q_b[0]), Q_SCALE))
            res = _norm_matmul(h, projs)
            k_sh, v_sh = res[0].reshape(b, s, -1), res[1].reshape(b, s, -1)
            q = res[2] if len(res) > 2 else None
    return h.reshape(b, s, d)
```

```python
import functools
import math

import numpy as np
import jax
import jax.numpy as jnp
from jax import lax
from jax.experimental import pallas as pl
from jax.experimental.pallas import tpu as pltpu

F32 = jnp.float32
BF16 = jnp.bfloat16

HEAD_DIM = 64
N_HEADS = 8
DIL_GROUPS = ((128, 1), (512, 4), (2048, 16))
ATTN_BLOCK = 128
NUM_BUCKETS = 32
MAX_DISTANCE = 2048
CONV_WIDTH = 3
RMS_EPS = 1e-6
SUBLN_EPS = 1e-5

LANES = 128
SUBLANES = 8
ROW_TILE = 512
BIAS_ROWS = 32
VMEM_LIMIT = 56 * 1024 * 1024

NEG_INF = float("-inf")
LOG2E = math.log2(math.e)
DIFF_HEADS = 4
Q_SCALE = HEAD_DIM ** -0.5 * LOG2E


def _cparams(sem):
    return pltpu.CompilerParams(dimension_semantics=sem, vmem_limit_bytes=VMEM_LIMIT)


def _bucket_np(dist):
    n = np.maximum(dist, 0)
    max_exact = NUM_BUCKETS // 2

    def large(dtype):
        nf = np.maximum(n, 1).astype(dtype)
        v = np.log(nf / dtype(max_exact)) / dtype(math.log(MAX_DISTANCE / max_exact)) * dtype(NUM_BUCKETS - max_exact)
        return np.minimum(max_exact + v.astype(np.int32), NUM_BUCKETS - 1)

    l32, l64 = large(np.float32), large(np.float64)
    assert np.array_equal(l32, l64), "bucket boundary is rounding sensitive"
    return np.where(n < max_exact, n, l32).astype(np.int32)


def _bias_kernel(tbl_ref, idx_ref, o_ref, *, plan, scale):
    h = pl.program_id(0)
    for t, blocks in enumerate(plan):
        for rb, (buckets, has_mask) in enumerate(blocks):
            rows = slice(rb * BIAS_ROWS, (rb + 1) * BIAS_ROWS)
            shape = (BIAS_ROWS, idx_ref.shape[-1])
            if len(buckets) == 1 and not has_mask:
                o_ref[t, rows, :] = jnp.full(shape, tbl_ref[h, buckets[0]] * scale, F32)
                continue
            idx = idx_ref[t, rows, :]
            acc = jnp.full(shape, NEG_INF, F32)
            for k in buckets:
                acc = jnp.where(idx == k, tbl_ref[h, k] * scale, acc)
            o_ref[t, rows, :] = acc


def _bias_tiles(table, idx, head_major, scale=1.0):
    t, r, c = idx.shape
    nh = table.shape[0]
    plan = tuple(
        tuple((tuple(int(k) for k in np.unique(blk[blk >= 0])), bool((blk < 0).any()))
              for blk in idx[i].reshape(r // BIAS_ROWS, BIAS_ROWS * c))
        for i in range(t))
    if head_major:
        out_shape, out_block, out_map = (nh, t, r, c), (None, t, r, c), (lambda h: (h, 0, 0, 0))
    else:
        out_shape, out_block, out_map = (t, nh, r, c), (t, None, r, c), (lambda h: (0, h, 0, 0))
    return pl.pallas_call(
        functools.partial(_bias_kernel, plan=plan, scale=scale),
        grid=(nh,),
        in_specs=[pl.BlockSpec(memory_space=pltpu.SMEM), pl.BlockSpec((t, r, c), lambda h: (0, 0, 0))],
        out_specs=pl.BlockSpec(out_block, out_map),
        out_shape=jax.ShapeDtypeStruct(out_shape, F32),
        compiler_params=_cparams(("arbitrary",)),
        name="rel_bias_tiles",
    )(table, jnp.asarray(idx))


def _dilated_bias_idx():
    qi = np.arange(ATTN_BLOCK)[:, None]
    ki = np.arange(2 * ATTN_BLOCK)[None, :]
    dist_u = qi + ATTN_BLOCK - ki
    out = []
    for window, dil in DIL_GROUPS:
        band = (dist_u >= 0) & (dist_u <= window // dil)
        out.append(np.where(band, _bucket_np(dist_u * dil), -1))
    return np.stack(out).astype(np.int32)


def _diff_bias_idx(seq, tile):
    nq = seq // tile
    i = np.arange(tile)[:, None]
    j = np.arange(tile)[None, :]
    tiles = []
    for d in range(nq):
        dist = d * tile + i - j
        tiles.append(np.where(dist >= 0, _bucket_np(dist), -1).astype(np.int32))
    nd = nq
    while nd > 1 and np.array_equal(tiles[nd - 1], tiles[nd - 2]):
        nd -= 1
    assert all(np.array_equal(tiles[d], tiles[nd - 1]) for d in range(nd - 1, nq))
    far = np.unique(tiles[nd - 1])
    if nd >= 2 and far.size == 1 and far[0] >= 0:
        return np.stack(tiles[:nd - 1]), int(far[0])
    return np.stack(tiles[:nd]), None


def _rms(x, g, eps):
    return x * lax.rsqrt(jnp.mean(x * x, axis=-1, keepdims=True) + eps) * g


def _norm_matmul_kernel(x_ref, *refs, n_chunk, out_scales):
    n_proj = len(out_scales)
    x = x_ref[...]
    xs = x * lax.rsqrt(jnp.mean(x * x, axis=-1, keepdims=True) + RMS_EPS)
    for p in range(n_proj):
        g_ref, w_ref, o_ref = refs[2 * p], refs[2 * p + 1], refs[2 * n_proj + p]
        xn = (xs * g_ref[...]).astype(BF16)
        for j in range(o_ref.shape[-1] // n_chunk):
            sl = slice(j * n_chunk, (j + 1) * n_chunk)
            res = jnp.dot(xn, w_ref[:, sl], preferred_element_type=F32)
            if out_scales[p] != 1.0:
                res = res * out_scales[p]
            o_ref[:, sl] = res.astype(o_ref.dtype)


def _norm_matmul(x2d, projs, tm=ROW_TILE, n_chunk=512):
    m, d = x2d.shape
    ins, in_specs = [], []
    out_scales = tuple(float(sc) for _, _, sc in projs)
    projs = [(g, w) for g, w, _ in projs]
    for g, w in projs:
        ins += [g.reshape(1, d), w]
        in_specs += [pl.BlockSpec((1, d), lambda i: (0, 0)), pl.BlockSpec(w.shape, lambda i: (0, 0))]
    return pl.pallas_call(
        functools.partial(_norm_matmul_kernel, n_chunk=n_chunk, out_scales=out_scales),
        grid=(m // tm,),
        in_specs=[pl.BlockSpec((tm, d), lambda i: (i, 0))] + in_specs,
        out_specs=[pl.BlockSpec((tm, w.shape[1]), lambda i: (i, 0)) for _, w in projs],
        out_shape=[jax.ShapeDtypeStruct((m, w.shape[1]), BF16) for _, w in projs],
        compiler_params=_cparams(("arbitrary",)),
        name="norm_matmul",
    )(x2d, *ins)


def _proj_a_kernel(x_ref, g_ref, w_ref, *refs, n_chunk):
    out_refs, xn_sc = refs[:-1], refs[-1]
    tm = x_ref.shape[0]
    xn = _rms(x_ref[...], g_ref[...], RMS_EPS)
    nlb = xn.shape[1] // LANES
    for j in range(nlb):
        xn_sc[j] = xn[:, j * LANES:(j + 1) * LANES]
    gw = out_refs[0].shape[-1]
    for gi, (o_ref, (_, dil)) in enumerate(zip(out_refs, DIL_GROUPS)):
        per = tm // dil
        if dil == 1:
            xg = xn
        else:
            xg = jnp.concatenate(
                [jnp.concatenate([xn_sc[j, pl.ds(c, per, stride=dil), :] for c in range(dil)], axis=0)
                 for j in range(nlb)], axis=1)
        xg = xg.astype(BF16)
        for j in range(gw // n_chunk):
            res = jnp.dot(xg, w_ref[:, gi * gw + j * n_chunk:gi * gw + (j + 1) * n_chunk],
                          preferred_element_type=F32)
            if j == 0:
                res = res * Q_SCALE
            res = res.astype(o_ref.dtype)
            for c in range(dil):
                o_ref[c, :, j * n_chunk:(j + 1) * n_chunk] = res[c * per:(c + 1) * per, :]


def _proj_a(x2d, g, w_bf16, b, s, tm=ROW_TILE, n_chunk=512):
    m, d = x2d.shape
    gw = w_bf16.shape[1] // len(DIL_GROUPS)
    tps = s // tm
    assert all(tm % (16 * dil) == 0 for _, dil in DIL_GROUPS) and n_chunk == N_HEADS * HEAD_DIM
    return pl.pallas_call(
        functools.partial(_proj_a_kernel, n_chunk=n_chunk),
        grid=(b, tps),
        in_specs=[pl.BlockSpec((tm, d), lambda bb, i: (bb * tps + i, 0)),
                  pl.BlockSpec((1, d), lambda bb, i: (0, 0)),
                  pl.BlockSpec(w_bf16.shape, lambda bb, i: (0, 0))],
        out_specs=[pl.BlockSpec((None, dil, tm // dil, gw), lambda bb, i: (bb, 0, i, 0)) for _, dil in DIL_GROUPS],
        out_shape=[jax.ShapeDtypeStruct((b, dil, s // dil, gw), BF16) for _, dil in DIL_GROUPS],
        scratch_shapes=[pltpu.VMEM((d // LANES, tm, LANES), F32)],
        compiler_params=_cparams(("arbitrary", "arbitrary")),
        name="proj_dilated",
    )(x2d, g.reshape(1, d), w_bf16)


def _dil_kernel(q_ref, k_ref, v_ref, kp_ref, vp_ref, bias_ref, o_ref, lse_ref):
    blk = ATTN_BLOCK
    n_cls, tq = q_ref.shape[0], q_ref.shape[1]
    i = pl.program_id(2)
    lane = lax.broadcasted_iota(jnp.int32, (1, LANES), 1)
    col = lax.broadcasted_iota(jnp.int32, (blk, 2 * blk), 1)
    first_pen = jnp.where(jnp.logical_and(i == 0, col < blk), NEG_INF, 0.0).astype(F32)
    ones = jnp.ones((2 * blk, LANES), BF16)
    for cc in range(n_cls):
        for j in range(tq // blk):
            rows = slice(j * blk, (j + 1) * blk)
            qj = q_ref[cc, rows, :]
            if j == 0:
                kx = jnp.concatenate([kp_ref[cc], k_ref[cc, 0:blk, :]], axis=0)
                vx = jnp.concatenate([vp_ref[cc], v_ref[cc, 0:blk, :]], axis=0)
            else:
                kx = k_ref[cc, (j - 1) * blk:(j + 1) * blk, :]
                vx = v_ref[cc, (j - 1) * blk:(j + 1) * blk, :]
            lse_tile = jnp.zeros((blk, LANES), F32)
            for hp in range(N_HEADS // 2):
                cols = slice(hp * LANES, (hp + 1) * LANES)
                q2, k2 = qj[:, cols], kx[:, cols]
                v2 = jnp.concatenate([vx[:, cols], ones], axis=1)
                o_pair = None
                for e in range(2):
                    h = 2 * hp + e
                    sel = (lane >= HEAD_DIM) if e else (lane < HEAD_DIM)
                    qm = jnp.where(sel, q2, jnp.zeros_like(q2))
                    s = lax.dot_general(qm, k2, (((1,), (1,)), ((), ())), preferred_element_type=F32) + bias_ref[h]
                    if j == 0:
                        s = s + first_pen
                    m = jnp.max(s, axis=-1, keepdims=True)
                    p = jnp.exp2(s - m)
                    pv = jnp.dot(p.astype(BF16), v2, preferred_element_type=F32)
                    l = pv[:, LANES:]
                    o_e = pv[:, :LANES] * (1.0 / l)
                    o_pair = o_e if e == 0 else jnp.where(lane < HEAD_DIM, o_pair, o_e)
                    lse_tile = jnp.where(lane == h, m + jnp.log2(l), lse_tile)
                o_ref[cc, rows, cols] = o_pair.astype(o_ref.dtype)
            lse_ref[cc, rows, :] = lse_tile


def _dilated_group(qkv, bias, g, dil):
    b, r, n, _ = qkv.shape
    w = N_HEADS * HEAD_DIM
    assert r == dil and n % ATTN_BLOCK == 0
    tq = min(n, DIL_ROWS)
    n_cls = min(dil, DIL_ROWS // tq)
    sub = tq // ATTN_BLOCK

    def cur(which):
        return pl.BlockSpec((None, n_cls, tq, w), lambda bb, c, i: (bb, c, i, which))

    def prev(which):
        return pl.BlockSpec((None, n_cls, ATTN_BLOCK, w),
                            lambda bb, c, i: (bb, c, jnp.maximum(i * sub - 1, 0), which))

    return pl.pallas_call(
        _dil_kernel,
        grid=(b, dil // n_cls, n // tq),
        in_specs=[cur(0), cur(1), cur(2), prev(1), prev(2),
                  pl.BlockSpec((None, N_HEADS, ATTN_BLOCK, 2 * ATTN_BLOCK), lambda bb, c, i: (g, 0, 0, 0))],
        out_specs=[pl.BlockSpec((None, n_cls, tq, w), lambda bb, c, i: (bb, c, i, 0)),
                   pl.BlockSpec((None, n_cls, tq, LANES), lambda bb, c, i: (bb, c, i, 0))],
        out_shape=[jax.ShapeDtypeStruct((b, dil, n, w), BF16),
                   jax.ShapeDtypeStruct((b, dil, n, LANES), F32)],
        compiler_params=_cparams(("arbitrary", "arbitrary", "arbitrary")),
        name=f"dilated_attn_g{g}",
    )(qkv, qkv, qkv, qkv, qkv, bias)


def _combine_out_kernel(o0, o1, o2, l0, l1, l2, e_ref, w_ref, g_ref, h_ref, out_ref, o_sc, l_sc):
    tm = h_ref.shape[0]

    def token_order(o_ref, l_ref, slot):
        dil = o_ref.shape[0]
        if dil == 1:
            return o_ref[0].astype(F32), l_ref[0]
        per = tm // dil
        nlb = o_ref.shape[-1] // LANES
        for c in range(dil):
            oc = o_ref[c].astype(F32)
            for j in range(nlb):
                o_sc[slot, j, pl.ds(c, per, stride=dil), :] = oc[:, j * LANES:(j + 1) * LANES]
            l_sc[slot, pl.ds(c, per, stride=dil), :] = l_ref[c]
        return jnp.concatenate([o_sc[slot, j] for j in range(nlb)], axis=1), l_sc[slot]

    pairs = [token_order(o_ref, l_ref, slot) for slot, (o_ref, l_ref) in enumerate(((o0, l0), (o1, l1), (o2, l2)))]
    ls = [l for _, l in pairs]
    mx = jnp.maximum(jnp.maximum(ls[0], ls[1]), ls[2])
    ws = [jnp.exp2(l - mx) for l in ls]
    inv = 1.0 / (ws[0] + ws[1] + ws[2])
    e = e_ref[...]

    def expand(wg):
        rem = wg * inv
        ax = None
        for _ in range(3):
            piece = rem.astype(BF16)
            rem = rem - piece.astype(F32)
            part = jnp.dot(piece, e, preferred_element_type=F32)
            ax = part if ax is None else ax + part
        return ax

    o_last = pairs[-1][0]
    a = o_last
    for (og, _), wg in zip(pairs[:-1], ws[:-1]):
        a = a + expand(wg) * (og - o_last)
    y = jnp.dot(a.astype(BF16), w_ref[...], preferred_element_type=F32)
    out_ref[...] = h_ref[...] + _rms(y, g_ref[...], RMS_EPS)


def _combine_out(os_, lses, w_bf16, g, h2d, b, s, tm=ROW_TILE):
    m, d = h2d.shape
    w = os_[0].shape[-1]
    tps = s // tm
    expand = np.zeros((LANES, w), np.float32)
    for hh in range(N_HEADS):
        expand[hh, hh * HEAD_DIM:(hh + 1) * HEAD_DIM] = 1.0
    grp = lambda a: pl.BlockSpec((None, a.shape[1], tm // a.shape[1], a.shape[3]), lambda bb, i: (bb, 0, i, 0))
    row = pl.BlockSpec((tm, d), lambda bb, i: (bb * tps + i, 0))
    full = lambda shp: pl.BlockSpec(shp, lambda bb, i: (0, 0))
    return pl.pallas_call(
        _combine_out_kernel,
        grid=(b, tps),
        in_specs=[grp(a) for a in os_] + [grp(a) for a in lses]
                 + [full((LANES, w)), full((w, d)), full((1, d)), row],
        out_specs=row,
        out_shape=jax.ShapeDtypeStruct((m, d), F32),
        scratch_shapes=[pltpu.VMEM((3, w // LANES, tm, LANES), F32), pltpu.VMEM((3, tm, LANES), F32)],
        compiler_params=_cparams(("arbitrary", "arbitrary")),
        name="combine_out_proj",
    )(*os_, *lses, jnp.asarray(expand, BF16), w_bf16, g.reshape(1, d), h2d)


def _ffn_kernel(*refs, tiles_per_seq, f_chunk, fuse_out_proj):
    if fuse_out_proj:
        a_ref, wo_ref, g_mix_ref, refs = refs[0], refs[1], refs[2], refs[3:]
    h_ref, g_in_ref, wup_ref, cw_ref, cb_ref, wdn_ref, g_out_ref, out_ref, h_sc, xn_sc, halo_sc, acc_sc = refs[:12]
    u_bufs = refs[12:]
    tm = h_ref.shape[0]
    d_ff = wdn_ref.shape[0]
    nc = d_ff // f_chunk
    i = pl.program_id(0)
    if fuse_out_proj:
        mix = jnp.dot(a_ref[...], wo_ref[...], preferred_element_type=F32)
        h_sc[...] = h_ref[...] + _rms(mix, g_mix_ref[...], RMS_EPS)
    else:
        h_sc[...] = h_ref[...]
    xn_sc[...] = _rms(h_sc[...], g_in_ref[...], RMS_EPS).astype(BF16)
    acc_sc[...] = jnp.zeros(acc_sc.shape, F32)
    seq_start = (i % tiles_per_seq) == 0
    inv_sqrt2 = 1.0 / math.sqrt(2.0)

    def cols(c, j):
        return slice(j * d_ff + c * f_chunk, j * d_ff + (c + 1) * f_chunk)

    def up(c, u_sc):
        for j in range(2):
            u_sc[j, SUBLANES:tm + SUBLANES, :] = jnp.dot(xn_sc[...], wup_ref[:, cols(c, j)],
                                                         preferred_element_type=F32)

    def conv(u_sc, c, j):
        cw = cw_ref[:, cols(c, j)]
        out = cb_ref[:, cols(c, j)]
        for t in range(CONV_WIDTH):
            off = SUBLANES - (CONV_WIDTH - 1 - t)
            out = out + cw[t:t + 1, :] * u_sc[j, off:tm + off, :]
        return out

    def down(c, u_sc):
        u_sc[:, 0:SUBLANES, :] = jnp.where(seq_start, 0.0, halo_sc[c])
        halo_sc[c] = u_sc[:, tm:tm + SUBLANES, :]
        gate = conv(u_sc, c, 0)
        val = conv(u_sc, c, 1)
        act = (0.5 * gate * (1.0 + lax.erf(gate * inv_sqrt2)) * val).astype(BF16)
        acc_sc[...] += jnp.dot(act, wdn_ref[c * f_chunk:(c + 1) * f_chunk, :], preferred_element_type=F32)

    for c in range(min(FFN_AHEAD, nc)):
        up(c, u_bufs[c])
    for c in range(nc):
        if c + FFN_AHEAD < nc:
            up(c + FFN_AHEAD, u_bufs[(c + FFN_AHEAD) % (FFN_AHEAD + 1)])
        down(c, u_bufs[c % (FFN_AHEAD + 1)])
    out_ref[...] = h_sc[...] + _rms(acc_sc[...], g_out_ref[...], RMS_EPS)


def _conv_ffn(h2d, g_in, w_up_bf16, conv_w, conv_b, w_down_bf16, g_out, seq, out_proj=None, tm=ROW_TILE, f_chunk=256):
    m, d = h2d.shape
    f2 = w_up_bf16.shape[1]
    d_ff = f2 // 2
    assert seq % tm == 0 and d_ff % f_chunk == 0 and CONV_WIDTH - 1 <= SUBLANES
    nc = d_ff // f_chunk
    full = lambda shp: pl.BlockSpec(shp, lambda i: (0, 0))
    row = lambda width: pl.BlockSpec((tm, width), lambda i: (i, 0))
    ins = [h2d, g_in.reshape(1, d), w_up_bf16, conv_w, conv_b.reshape(1, f2), w_down_bf16, g_out.reshape(1, d)]
    in_specs = [row(d), full((1, d)), full((d, f2)), full((CONV_WIDTH, f2)), full((1, f2)), full((d_ff, d)),
                full((1, d))]
    if out_proj is not None:
        a2d, w_out_bf16, g_mix = out_proj
        ins = [a2d, w_out_bf16, g_mix.reshape(1, d)] + ins
        in_specs = [row(a2d.shape[1]), full(w_out_bf16.shape), full((1, d))] + in_specs
    return pl.pallas_call(
        functools.partial(_ffn_kernel, tiles_per_seq=seq // tm, f_chunk=f_chunk, fuse_out_proj=out_proj is not None),
        grid=(m // tm,),
        in_specs=in_specs,
        out_specs=row(d),
        out_shape=jax.ShapeDtypeStruct((m, d), F32),
        scratch_shapes=[pltpu.VMEM((tm, d), F32), pltpu.VMEM((tm, d), BF16),
                        pltpu.VMEM((nc, 2, SUBLANES, f_chunk), F32), pltpu.VMEM((tm, d), F32)]
                       + [pltpu.VMEM((2, tm + SUBLANES, f_chunk), F32)] * (FFN_AHEAD + 1),
        compiler_params=_cparams(("arbitrary",)),
        name="conv_ffn",
    )(*ins)


def _diff_kernel(tbl_ref, q_ref, k_ref, v_ref, bias_ref, lq1, lk1, lq2, lk2, sg_ref, o_ref,
                 qs_sc, s_sc, m_sc, l_sc, acc_sc, *, tile, n_bias, far_bucket, lambda_init, heads):
    hg = pl.program_id(0)
    qi = pl.program_id(2)
    hd2 = 2 * HEAD_DIM
    lane = lax.broadcasted_iota(jnp.int32, (1, LANES), 1)
    for hh in range(heads):
        q = q_ref[:, hh * hd2:(hh + 1) * hd2]
        zero = jnp.zeros_like(q)
        qs_sc[hh, 0:tile, :] = jnp.where(lane < HEAD_DIM, q, zero)
        qs_sc[hh, tile:2 * tile, :] = jnp.where(lane >= HEAD_DIM, q, zero)
    m_sc[...] = jnp.full(m_sc.shape, NEG_INF, F32)
    l_sc[...] = jnp.zeros(l_sc.shape, F32)
    acc_sc[...] = jnp.zeros(acc_sc.shape, F32)
    ones = jnp.ones((tile, LANES), BF16)

    def scores(ki, hh, c):
        k = k_ref[pl.ds(pl.multiple_of(ki * tile, tile), tile), hh * hd2:(hh + 1) * hd2]
        return lax.dot_general(qs_sc[hh, c * tile:(c + 1) * tile, :], k, (((1,), (1,)), ((), ())),
                               preferred_element_type=F32)

    def update(hh, rows, s, vx):
        m_prev = m_sc[hh, rows, :]
        m_new = jnp.maximum(m_prev, jnp.max(s, axis=-1, keepdims=True))
        alpha = jnp.exp2(m_prev - m_new)
        p = jnp.exp2(s - jnp.concatenate([m_new] * (s.shape[1] // LANES), axis=1))
        pv = jnp.dot(p.astype(BF16), vx, preferred_element_type=F32)
        l_sc[hh, rows, :] = alpha * l_sc[hh, rows, :] + pv[:, hd2:]
        acc_sc[hh, rows, :] = alpha * acc_sc[hh, rows, :] + pv[:, :hd2]
        m_sc[hh, rows, :] = m_new

    def values(ki, hh):
        v = v_ref[pl.ds(pl.multiple_of(ki * tile, tile), tile), hh * hd2:(hh + 1) * hd2]
        return jnp.concatenate([v, ones], axis=1)

    def step(ki, far):
        for hh in range(heads):
            vx = values(ki, hh)
            if far:
                bias = tbl_ref[hg * heads + hh, far_bucket] * LOG2E
            else:
                bias = bias_ref[hh, jnp.minimum(qi - ki, n_bias - 1)]
            for c in range(2):
                rows = slice(c * tile, (c + 1) * tile)
                s = s_sc[hh, rows, :] + bias
                s_sc[hh, rows, :] = scores(ki + 1, hh, c)
                update(hh, rows, s, vx)

    def diagonal_step():
        half = tile // 2
        for hh in range(heads):
            vx = values(qi, hh)
            for c in range(2):
                for r0 in range(0, tile, half):
                    rows = slice(c * tile + r0, c * tile + r0 + half)
                    nk = r0 + half
                    s = s_sc[hh, rows, 0:nk] + bias_ref[hh, 0, r0:r0 + half, 0:nk]
                    update(hh, rows, s, vx[0:nk, :])

    for hh in range(heads):
        for c in range(2):
            s_sc[hh, c * tile:(c + 1) * tile, :] = scores(0, hh, c)

    def far_body(ki, carry):
        step(ki, True)
        return carry

    def near_body(ki, carry):
        step(ki, False)
        return carry

    if far_bucket is None:
        n_far = 0
    else:
        n_far = jnp.maximum(qi - (n_bias - 1), 0)
        lax.fori_loop(0, n_far, far_body, 0)
    lax.fori_loop(n_far, qi, near_body, 0)
    diagonal_step()

    lam = (jnp.exp(jnp.sum(lq1[...].astype(F32) * lk1[...].astype(F32), axis=-1, keepdims=True))
           - jnp.exp(jnp.sum(lq2[...].astype(F32) * lk2[...].astype(F32), axis=-1, keepdims=True))
           + lambda_init)
    for hh in range(heads):
        o = acc_sc[hh] * (1.0 / l_sc[hh])
        a = o[0:tile, :] - lam * o[tile:2 * tile, :]
        y = _rms(a, sg_ref[...], SUBLN_EPS) * (1.0 - lambda_init)
        o_ref[:, hh * hd2:(hh + 1) * hd2] = y.astype(o_ref.dtype)


def _diff_attention(q, k, v, table, bias, far_bucket, lq1, lk1, lq2, lk2, subln_g, lambda_init, tile, heads=DIFF_HEADS):
    b, s, _ = q.shape
    n_bias = bias.shape[1]
    hd2 = 2 * HEAD_DIM
    hw = heads * hd2
    ng = N_HEADS // heads
    vec = lambda a: a.reshape(1, -1)
    small = lambda n: pl.BlockSpec((1, n), lambda h, bb, i: (0, 0))
    return pl.pallas_call(
        functools.partial(_diff_kernel, tile=tile, n_bias=n_bias, far_bucket=far_bucket, lambda_init=lambda_init,
                          heads=heads),
        grid=(ng, b, s // tile),
        in_specs=[pl.BlockSpec(memory_space=pltpu.SMEM),
                  pl.BlockSpec((None, tile, hw), lambda h, bb, i: (bb, i, h)),
                  pl.BlockSpec((None, s, hw), lambda h, bb, i: (bb, 0, h)),
                  pl.BlockSpec((None, s, hw), lambda h, bb, i: (bb, 0, h)),
                  pl.BlockSpec((heads, n_bias, tile, tile), lambda h, bb, i: (h, 0, 0, 0),
                               pipeline_mode=pl.Buffered(1)),
                  small(HEAD_DIM), small(HEAD_DIM), small(HEAD_DIM), small(HEAD_DIM), small(hd2)],
        out_specs=pl.BlockSpec((None, tile, hw), lambda h, bb, i: (bb, i, h)),
        out_shape=jax.ShapeDtypeStruct((b, s, N_HEADS * hd2), BF16),
        scratch_shapes=[pltpu.VMEM((heads, 2 * tile, hd2), BF16), pltpu.VMEM((heads, 2 * tile, tile), F32),
                        pltpu.VMEM((heads, 2 * tile, LANES), F32), pltpu.VMEM((heads, 2 * tile, LANES), F32),
                        pltpu.VMEM((heads, 2 * tile, hd2), F32)],
        compiler_params=_cparams(("arbitrary", "arbitrary", "arbitrary")),
        name="diff_attn",
    )(table, q, k, v, bias, vec(lq1), vec(lk1), vec(lq2), vec(lk2), vec(subln_g))


DIFF_TILE = 512
DIL_ROWS = 2048
FFN_AHEAD = 6


def kernel(x, rel_bias_table, norm_g, w_in_a, w_out_a, kv_norm_g, w_k_shared, w_v_shared, w_q_b,
           lam_q1, lam_k1, lam_q2, lam_k2, subln_g, w_out_b, w_up, conv_w, conv_b, w_down):
    b, s, d = x.shape
    depth = norm_g.shape[0]
    n_a = w_in_a.shape[0]
    bf = lambda a: a.astype(BF16)

    dil_bias = _bias_tiles(rel_bias_table, _dilated_bias_idx(), head_major=False, scale=LOG2E)
    diff_tile = min(DIFF_TILE, s)
    diff_idx, far_bucket = _diff_bias_idx(s, diff_tile)
    diff_bias = _bias_tiles(rel_bias_table, diff_idx, head_major=True, scale=LOG2E)

    h = x.reshape(b * s, d)
    k_sh = v_sh = q = None
    for layer in range(depth):
        g = norm_g[layer]
        if layer < n_a:
            qkvs = _proj_a(h, g[0], bf(w_in_a[layer]), b, s)
            outs = [_dilated_group(qkvs[gi], dil_bias, gi, dil) for gi, (_, dil) in enumerate(DIL_GROUPS)]
            h = _combine_out([o for o, _ in outs], [l for _, l in outs], bf(w_out_a[layer]), g[1], h, b, s)
            mixer_out = None
        else:
            j = layer - n_a
            lambda_init = 0.8 - 0.6 * math.exp(-0.3 * layer)
            if q is None:
                q, = _norm_matmul(h, [(g[0], bf(w_q_b[j]), Q_SCALE)])
            a = _diff_attention(q.reshape(b, s, -1), k_sh, v_sh, rel_bias_table, diff_bias, far_bucket, lam_q1[j],
                                lam_k1[j], lam_q2[j], lam_k2[j], subln_g[j], lambda_init, diff_tile)
            q = None
            mixer_out = (a.reshape(b * s, -1), bf(w_out_b[j]), g[1])
        h = _conv_ffn(h, g[2], bf(w_up[layer]), conv_w[layer], conv_b[layer], bf(w_down[layer]), g[3], s,
                      out_proj=mixer_out)
        if layer == n_a - 1:
            projs = [(kv_norm_g, bf(w_k_shared), 1.0), (kv_norm_g, bf(w_v_shared), 1.0)]
            if layer + 1 < depth:
                projs.append((norm_g[layer + 1, 0], bf(w_q_b[0]), Q_SCALE))
            res = _norm_matmul(h, projs)
            k_sh, v_sh = res[0].reshape(b, s, -1), res[1].reshape(b, s, -1)
            q = res[2] if len(res) > 2 else None
    return h.reshape(b, s, d)
```

```python
import functools
import math

import numpy as np
import jax
import jax.numpy as jnp
from jax import lax
from jax.experimental import pallas as pl
from jax.experimental.pallas import tpu as pltpu

F32 = jnp.float32
BF16 = jnp.bfloat16

HEAD_DIM = 64
N_HEADS = 8
DIL_GROUPS = ((128, 1), (512, 4), (2048, 16))
ATTN_BLOCK = 128
NUM_BUCKETS = 32
MAX_DISTANCE = 2048
CONV_WIDTH = 3
RMS_EPS = 1e-6
SUBLN_EPS = 1e-5

LANES = 128
SUBLANES = 8
ROW_TILE = 512
BIAS_ROWS = 32
VMEM_LIMIT = 56 * 1024 * 1024

NEG_INF = float("-inf")
LOG2E = math.log2(math.e)
DIFF_TILE = 512
DIFF_HEADS = 4
DIL_ROWS = 2048
FFN_AHEAD = 6
Q_SCALE = HEAD_DIM ** -0.5 * LOG2E


def _cparams(sem):
    return pltpu.CompilerParams(dimension_semantics=sem, vmem_limit_bytes=VMEM_LIMIT)


def _bucket_np(dist):
    n = np.maximum(dist, 0)
    max_exact = NUM_BUCKETS // 2

    def large(dtype):
        nf = np.maximum(n, 1).astype(dtype)
        v = np.log(nf / dtype(max_exact)) / dtype(math.log(MAX_DISTANCE / max_exact)) * dtype(NUM_BUCKETS - max_exact)
        return np.minimum(max_exact + v.astype(np.int32), NUM_BUCKETS - 1)

    l32, l64 = large(np.float32), large(np.float64)
    assert np.array_equal(l32, l64), "bucket boundary is rounding sensitive"
    return np.where(n < max_exact, n, l32).astype(np.int32)


def _bias_kernel(tbl_ref, idx_ref, o_ref, *, plan, scale):
    h = pl.program_id(0)
    for t, blocks in enumerate(plan):
        for rb, (buckets, has_mask) in enumerate(blocks):
            rows = slice(rb * BIAS_ROWS, (rb + 1) * BIAS_ROWS)
            shape = (BIAS_ROWS, idx_ref.shape[-1])
            if len(buckets) == 1 and not has_mask:
                o_ref[t, rows, :] = jnp.full(shape, tbl_ref[h, buckets[0]] * scale, F32)
                continue
            idx = idx_ref[t, rows, :]
            acc = jnp.full(shape, NEG_INF, F32)
            for k in buckets:
                acc = jnp.where(idx == k, tbl_ref[h, k] * scale, acc)
            o_ref[t, rows, :] = acc


def _bias_tiles(table, idx, head_major, scale=1.0):
    t, r, c = idx.shape
    nh = table.shape[0]
    plan = tuple(
        tuple((tuple(int(k) for k in np.unique(blk[blk >= 0])), bool((blk < 0).any()))
              for blk in idx[i].reshape(r // BIAS_ROWS, BIAS_ROWS * c))
        for i in range(t))
    if head_major:
        out_shape, out_block, out_map = (nh, t, r, c), (None, t, r, c), (lambda h: (h, 0, 0, 0))
    else:
        out_shape, out_block, out_map = (t, nh, r, c), (t, None, r, c), (lambda h: (0, h, 0, 0))
    return pl.pallas_call(
        functools.partial(_bias_kernel, plan=plan, scale=scale),
        grid=(nh,),
        in_specs=[pl.BlockSpec(memory_space=pltpu.SMEM), pl.BlockSpec((t, r, c), lambda h: (0, 0, 0))],
        out_specs=pl.BlockSpec(out_block, out_map),
        out_shape=jax.ShapeDtypeStruct(out_shape, F32),
        compiler_params=_cparams(("arbitrary",)),
        name="rel_bias_tiles",
    )(table, jnp.asarray(idx))


def _dilated_bias_idx():
    qi = np.arange(ATTN_BLOCK)[:, None]
    ki = np.arange(2 * ATTN_BLOCK)[None, :]
    dist_u = qi + ATTN_BLOCK - ki
    out = []
    for window, dil in DIL_GROUPS:
        band = (dist_u >= 0) & (dist_u <= window // dil)
        out.append(np.where(band, _bucket_np(dist_u * dil), -1))
    return np.stack(out).astype(np.int32)


def _diff_bias_idx(seq, tile):
    nq = seq // tile
    i = np.arange(tile)[:, None]
    j = np.arange(tile)[None, :]
    tiles = []
    for d in range(nq):
        dist = d * tile + i - j
        tiles.append(np.where(dist >= 0, _bucket_np(dist), -1).astype(np.int32))
    nd = nq
    while nd > 1 and np.array_equal(tiles[nd - 1], tiles[nd - 2]):
        nd -= 1
    assert all(np.array_equal(tiles[d], tiles[nd - 1]) for d in range(nd - 1, nq))
    far = np.unique(tiles[nd - 1])
    if nd >= 2 and far.size == 1 and far[0] >= 0:
        return np.stack(tiles[:nd - 1]), int(far[0])
    return np.stack(tiles[:nd]), None


def _rms(x, g, eps):
    return x * lax.rsqrt(jnp.mean(x * x, axis=-1, keepdims=True) + eps) * g


def _norm_matmul_kernel(x_ref, *refs, n_chunk, out_scales):
    n_proj = len(out_scales)
    x = x_ref[...]
    xs = x * lax.rsqrt(jnp.mean(x * x, axis=-1, keepdims=True) + RMS_EPS)
    for p in range(n_proj):
        g_ref, w_ref, o_ref = refs[2 * p], refs[2 * p + 1], refs[2 * n_proj + p]
        xn = (xs * g_ref[...]).astype(BF16)
        for j in range(o_ref.shape[-1] // n_chunk):
            sl = slice(j * n_chunk, (j + 1) * n_chunk)
            res = jnp.dot(xn, w_ref[:, sl].astype(BF16), preferred_element_type=F32)
            if out_scales[p] != 1.0:
                res = res * out_scales[p]
            o_ref[:, sl] = res.astype(o_ref.dtype)


def _norm_matmul(x2d, projs, tm=ROW_TILE, n_chunk=512):
    m, d = x2d.shape
    ins, in_specs = [], []
    out_scales = tuple(float(sc) for _, _, sc in projs)
    projs = [(g, w) for g, w, _ in projs]
    for g, w in projs:
        ins += [g.reshape(1, d), w]
        in_specs += [pl.BlockSpec((1, d), lambda i: (0, 0)), pl.BlockSpec(w.shape, lambda i: (0, 0))]
    return pl.pallas_call(
        functools.partial(_norm_matmul_kernel, n_chunk=n_chunk, out_scales=out_scales),
        grid=(m // tm,),
        in_specs=[pl.BlockSpec((tm, d), lambda i: (i, 0))] + in_specs,
        out_specs=[pl.BlockSpec((tm, w.shape[1]), lambda i: (i, 0)) for _, w in projs],
        out_shape=[jax.ShapeDtypeStruct((m, w.shape[1]), BF16) for _, w in projs],
        compiler_params=_cparams(("arbitrary",)),
        name="norm_matmul",
    )(x2d, *ins)


def _proj_a_kernel(x_ref, g_ref, w_ref, *refs, n_chunk):
    out_refs, xn_sc = refs[:-1], refs[-1]
    tm = x_ref.shape[0]
    xn = _rms(x_ref[...], g_ref[...], RMS_EPS)
    nlb = xn.shape[1] // LANES
    for j in range(nlb):
        xn_sc[j] = xn[:, j * LANES:(j + 1) * LANES]
    gw = out_refs[0].shape[-1]
    for gi, (o_ref, (_, dil)) in enumerate(zip(out_refs, DIL_GROUPS)):
        per = tm // dil
        if dil == 1:
            xg = xn
        else:
            xg = jnp.concatenate(
                [jnp.concatenate([xn_sc[j, pl.ds(c, per, stride=dil), :] for c in range(dil)], axis=0)
                 for j in range(nlb)], axis=1)
        xg = xg.astype(BF16)
        for j in range(gw // n_chunk):
            res = jnp.dot(xg, w_ref[:, gi * gw + j * n_chunk:gi * gw + (j + 1) * n_chunk].astype(BF16),
                          preferred_element_type=F32)
            if j == 0:
                res = res * Q_SCALE
            res = res.astype(o_ref.dtype)
            for c in range(dil):
                o_ref[c, :, j * n_chunk:(j + 1) * n_chunk] = res[c * per:(c + 1) * per, :]


def _proj_a(x2d, g, w, b, s, tm=ROW_TILE, n_chunk=512):
    m, d = x2d.shape
    gw = w.shape[1] // len(DIL_GROUPS)
    tps = s // tm
    assert all(tm % (16 * dil) == 0 for _, dil in DIL_GROUPS) and n_chunk == N_HEADS * HEAD_DIM
    return pl.pallas_call(
        functools.partial(_proj_a_kernel, n_chunk=n_chunk),
        grid=(b, tps),
        in_specs=[pl.BlockSpec((tm, d), lambda bb, i: (bb * tps + i, 0)),
                  pl.BlockSpec((1, d), lambda bb, i: (0, 0)),
                  pl.BlockSpec(w.shape, lambda bb, i: (0, 0))],
        out_specs=[pl.BlockSpec((None, dil, tm // dil, gw), lambda bb, i: (bb, 0, i, 0)) for _, dil in DIL_GROUPS],
        out_shape=[jax.ShapeDtypeStruct((b, dil, s // dil, gw), BF16) for _, dil in DIL_GROUPS],
        scratch_shapes=[pltpu.VMEM((d // LANES, tm, LANES), F32)],
        compiler_params=_cparams(("arbitrary", "arbitrary")),
        name="proj_dilated",
    )(x2d, g.reshape(1, d), w)


def _dil_kernel(q_ref, k_ref, v_ref, kp_ref, vp_ref, bias_ref, o_ref, lse_ref):
    blk = ATTN_BLOCK
    n_cls, tq = q_ref.shape[0], q_ref.shape[1]
    i = pl.program_id(2)
    lane = lax.broadcasted_iota(jnp.int32, (1, LANES), 1)
    col = lax.broadcasted_iota(jnp.int32, (blk, 2 * blk), 1)
    first_pen = jnp.where(jnp.logical_and(i == 0, col < blk), NEG_INF, 0.0).astype(F32)
    ones = jnp.ones((2 * blk, LANES), BF16)
    for cc in range(n_cls):
        for j in range(tq // blk):
            rows = slice(j * blk, (j + 1) * blk)
            qj = q_ref[cc, rows, :]
            if j == 0:
                kx = jnp.concatenate([kp_ref[cc], k_ref[cc, 0:blk, :]], axis=0)
                vx = jnp.concatenate([vp_ref[cc], v_ref[cc, 0:blk, :]], axis=0)
            else:
                kx = k_ref[cc, (j - 1) * blk:(j + 1) * blk, :]
                vx = v_ref[cc, (j - 1) * blk:(j + 1) * blk, :]
            lse_tile = jnp.zeros((blk, LANES), F32)
            for hp in range(N_HEADS // 2):
                cols = slice(hp * LANES, (hp + 1) * LANES)
                q2, k2 = qj[:, cols], kx[:, cols]
                v2 = jnp.concatenate([vx[:, cols], ones], axis=1)
                o_pair = None
                for e in range(2):
                    h = 2 * hp + e
                    sel = (lane >= HEAD_DIM) if e else (lane < HEAD_DIM)
                    qm = jnp.where(sel, q2, jnp.zeros_like(q2))
                    s = lax.dot_general(qm, k2, (((1,), (1,)), ((), ())), preferred_element_type=F32) + bias_ref[h]
                    if j == 0:
                        s = s + first_pen
                    m = jnp.max(s, axis=-1, keepdims=True)
                    p = jnp.exp2(s - m)
                    pv = jnp.dot(p.astype(BF16), v2, preferred_element_type=F32)
                    l = pv[:, LANES:]
                    o_e = pv[:, :LANES] * (1.0 / l)
                    o_pair = o_e if e == 0 else jnp.where(lane < HEAD_DIM, o_pair, o_e)
                    lse_tile = jnp.where(lane == h, m + jnp.log2(l), lse_tile)
                o_ref[cc, rows, cols] = o_pair.astype(o_ref.dtype)
            lse_ref[cc, rows, :] = lse_tile


def _dilated_group(qkv, bias, g, dil):
    b, r, n, _ = qkv.shape
    w = N_HEADS * HEAD_DIM
    assert r == dil and n % ATTN_BLOCK == 0
    tq = min(n, DIL_ROWS)
    n_cls = min(dil, DIL_ROWS // tq)
    sub = tq // ATTN_BLOCK

    def cur(which):
        return pl.BlockSpec((None, n_cls, tq, w), lambda bb, c, i: (bb, c, i, which))

    def prev(which):
        return pl.BlockSpec((None, n_cls, ATTN_BLOCK, w),
                            lambda bb, c, i: (bb, c, jnp.maximum(i * sub - 1, 0), which))

    return pl.pallas_call(
        _dil_kernel,
        grid=(b, dil // n_cls, n // tq),
        in_specs=[cur(0), cur(1), cur(2), prev(1), prev(2),
                  pl.BlockSpec((None, N_HEADS, ATTN_BLOCK, 2 * ATTN_BLOCK), lambda bb, c, i: (g, 0, 0, 0))],
        out_specs=[pl.BlockSpec((None, n_cls, tq, w), lambda bb, c, i: (bb, c, i, 0)),
                   pl.BlockSpec((None, n_cls, tq, LANES), lambda bb, c, i: (bb, c, i, 0))],
        out_shape=[jax.ShapeDtypeStruct((b, dil, n, w), BF16),
                   jax.ShapeDtypeStruct((b, dil, n, LANES), F32)],
        compiler_params=_cparams(("arbitrary", "arbitrary", "arbitrary")),
        name=f"dilated_attn_g{g}",
    )(qkv, qkv, qkv, qkv, qkv, bias)


def _combine_out_kernel(o0, o1, o2, l0, l1, l2, e_ref, w_ref, g_ref, h_ref, out_ref, o_sc, l_sc):
    tm = h_ref.shape[0]

    def token_order(o_ref, l_ref, slot):
        dil = o_ref.shape[0]
        if dil == 1:
            return o_ref[0].astype(F32), l_ref[0]
        per = tm // dil
        nlb = o_ref.shape[-1] // LANES
        for c in range(dil):
            oc = o_ref[c].astype(F32)
            for j in range(nlb):
                o_sc[slot, j, pl.ds(c, per, stride=dil), :] = oc[:, j * LANES:(j + 1) * LANES]
            l_sc[slot, pl.ds(c, per, stride=dil), :] = l_ref[c]
        return jnp.concatenate([o_sc[slot, j] for j in range(nlb)], axis=1), l_sc[slot]

    pairs = [token_order(o_ref, l_ref, slot) for slot, (o_ref, l_ref) in enumerate(((o0, l0), (o1, l1), (o2, l2)))]
    ls = [l for _, l in pairs]
    mx = jnp.maximum(jnp.maximum(ls[0], ls[1]), ls[2])
    ws = [jnp.exp2(l - mx) for l in ls]
    inv = 1.0 / (ws[0] + ws[1] + ws[2])
    e = e_ref[...]

    def expand(wg):
        rem = wg * inv
        ax = None
        for _ in range(3):
            piece = rem.astype(BF16)
            rem = rem - piece.astype(F32)
            part = jnp.dot(piece, e, preferred_element_type=F32)
            ax = part if ax is None else ax + part
        return ax

    o_last = pairs[-1][0]
    a = o_last
    for (og, _), wg in zip(pairs[:-1], ws[:-1]):
        a = a + expand(wg) * (og - o_last)
    y = jnp.dot(a.astype(BF16), w_ref[...].astype(BF16), preferred_element_type=F32)
    out_ref[...] = h_ref[...] + _rms(y, g_ref[...], RMS_EPS)


def _combine_out(os_, lses, w_out, g, h2d, b, s, tm=ROW_TILE):
    m, d = h2d.shape
    w = os_[0].shape[-1]
    tps = s // tm
    expand = np.zeros((LANES, w), np.float32)
    for hh in range(N_HEADS):
        expand[hh, hh * HEAD_DIM:(hh + 1) * HEAD_DIM] = 1.0
    grp = lambda a: pl.BlockSpec((None, a.shape[1], tm // a.shape[1], a.shape[3]), lambda bb, i: (bb, 0, i, 0))
    row = pl.BlockSpec((tm, d), lambda bb, i: (bb * tps + i, 0))
    full = lambda shp: pl.BlockSpec(shp, lambda bb, i: (0, 0))
    return pl.pallas_call(
        _combine_out_kernel,
        grid=(b, tps),
        in_specs=[grp(a) for a in os_] + [grp(a) for a in lses]
                 + [full((LANES, w)), full((w, d)), full((1, d)), row],
        out_specs=row,
        out_shape=jax.ShapeDtypeStruct((m, d), F32),
        scratch_shapes=[pltpu.VMEM((3, w // LANES, tm, LANES), F32), pltpu.VMEM((3, tm, LANES), F32)],
        compiler_params=_cparams(("arbitrary", "arbitrary")),
        name="combine_out_proj",
    )(*os_, *lses, jnp.asarray(expand, BF16), w_out, g.reshape(1, d), h2d)


def _ffn_kernel(*refs, tiles_per_seq, f_chunk, fuse_out_proj):
    if fuse_out_proj:
        a_ref, wo_ref, g_mix_ref, refs = refs[0], refs[1], refs[2], refs[3:]
    h_ref, g_in_ref, wup_ref, cw_ref, cb_ref, wdn_ref, g_out_ref, out_ref, h_sc, xn_sc, halo_sc, acc_sc = refs[:12]
    u_bufs = refs[12:]
    tm = h_ref.shape[0]
    d_ff = wdn_ref.shape[0]
    nc = d_ff // f_chunk
    i = pl.program_id(0)
    if fuse_out_proj:
        mix = jnp.dot(a_ref[...], wo_ref[...].astype(BF16), preferred_element_type=F32)
        h_sc[...] = h_ref[...] + _rms(mix, g_mix_ref[...], RMS_EPS)
    else:
        h_sc[...] = h_ref[...]
    xn_sc[...] = _rms(h_sc[...], g_in_ref[...], RMS_EPS).astype(BF16)
    acc_sc[...] = jnp.zeros(acc_sc.shape, F32)
    seq_start = (i % tiles_per_seq) == 0
    inv_sqrt2 = 1.0 / math.sqrt(2.0)

    def cols(c, j):
        return slice(j * d_ff + c * f_chunk, j * d_ff + (c + 1) * f_chunk)

    def up(c, u_sc):
        for j in range(2):
            u_sc[j, SUBLANES:tm + SUBLANES, :] = jnp.dot(xn_sc[...], wup_ref[:, cols(c, j)],
                                                         preferred_element_type=F32)

    def conv(u_sc, c, j):
        cw = cw_ref[:, cols(c, j)]
        out = cb_ref[:, cols(c, j)]
        for t in range(CONV_WIDTH):
            off = SUBLANES - (CONV_WIDTH - 1 - t)
            out = out + cw[t:t + 1, :] * u_sc[j, off:tm + off, :]
        return out

    def down(c, u_sc):
        u_sc[:, 0:SUBLANES, :] = jnp.where(seq_start, 0.0, halo_sc[c])
        halo_sc[c] = u_sc[:, tm:tm + SUBLANES, :]
        gate = conv(u_sc, c, 0)
        val = conv(u_sc, c, 1)
        act = (0.5 * gate * (1.0 + lax.erf(gate * inv_sqrt2)) * val).astype(BF16)
        acc_sc[...] += jnp.dot(act, wdn_ref[c * f_chunk:(c + 1) * f_chunk, :], preferred_element_type=F32)

    for c in range(min(FFN_AHEAD, nc)):
        up(c, u_bufs[c])
    for c in range(nc):
        if c + FFN_AHEAD < nc:
            up(c + FFN_AHEAD, u_bufs[(c + FFN_AHEAD) % (FFN_AHEAD + 1)])
        down(c, u_bufs[c % (FFN_AHEAD + 1)])
    out_ref[...] = h_sc[...] + _rms(acc_sc[...], g_out_ref[...], RMS_EPS)


def _conv_ffn(h2d, g_in, w_up_bf16, conv_w, conv_b, w_down_bf16, g_out, seq, out_proj=None, tm=ROW_TILE, f_chunk=256):
    m, d = h2d.shape
    f2 = w_up_bf16.shape[1]
    d_ff = f2 // 2
    assert seq % tm == 0 and d_ff % f_chunk == 0 and CONV_WIDTH - 1 <= SUBLANES
    nc = d_ff // f_chunk
    full = lambda shp: pl.BlockSpec(shp, lambda i: (0, 0))
    row = lambda width: pl.BlockSpec((tm, width), lambda i: (i, 0))
    ins = [h2d, g_in.reshape(1, d), w_up_bf16, conv_w, conv_b.reshape(1, f2), w_down_bf16, g_out.reshape(1, d)]
    in_specs = [row(d), full((1, d)), full((d, f2)), full((CONV_WIDTH, f2)), full((1, f2)), full((d_ff, d)),
                full((1, d))]
    if out_proj is not None:
        a2d, w_out, g_mix = out_proj
        ins = [a2d, w_out, g_mix.reshape(1, d)] + ins
        in_specs = [row(a2d.shape[1]), full(w_out.shape), full((1, d))] + in_specs
    return pl.pallas_call(
        functools.partial(_ffn_kernel, tiles_per_seq=seq // tm, f_chunk=f_chunk, fuse_out_proj=out_proj is not None),
        grid=(m // tm,),
        in_specs=in_specs,
        out_specs=row(d),
        out_shape=jax.ShapeDtypeStruct((m, d), F32),
        scratch_shapes=[pltpu.VMEM((tm, d), F32), pltpu.VMEM((tm, d), BF16),
                        pltpu.VMEM((nc, 2, SUBLANES, f_chunk), F32), pltpu.VMEM((tm, d), F32)]
                       + [pltpu.VMEM((2, tm + SUBLANES, f_chunk), F32)] * (FFN_AHEAD + 1),
        compiler_params=_cparams(("arbitrary",)),
        name="conv_ffn",
    )(*ins)


def _diff_kernel(tbl_ref, q_ref, k_ref, v_ref, bias_ref, lq1, lk1, lq2, lk2, sg_ref, o_ref,
                 qs_sc, s_sc, m_sc, l_sc, acc_sc, *, tile, n_bias, far_bucket, lambda_init, heads):
    hg = pl.program_id(0)
    qi = pl.program_id(2)
    hd2 = 2 * HEAD_DIM
    lane = lax.broadcasted_iota(jnp.int32, (1, LANES), 1)
    for hh in range(heads):
        q = q_ref[:, hh * hd2:(hh + 1) * hd2]
        zero = jnp.zeros_like(q)
        qs_sc[hh, 0:tile, :] = jnp.where(lane < HEAD_DIM, q, zero)
        qs_sc[hh, tile:2 * tile, :] = jnp.where(lane >= HEAD_DIM, q, zero)
    m_sc[...] = jnp.full(m_sc.shape, NEG_INF, F32)
    l_sc[...] = jnp.zeros(l_sc.shape, F32)
    acc_sc[...] = jnp.zeros(acc_sc.shape, F32)
    ones = jnp.ones((tile, LANES), BF16)

    def scores(ki, hh, c):
        k = k_ref[pl.ds(pl.multiple_of(ki * tile, tile), tile), hh * hd2:(hh + 1) * hd2]
        return lax.dot_general(qs_sc[hh, c * tile:(c + 1) * tile, :], k, (((1,), (1,)), ((), ())),
                               preferred_element_type=F32)

    def update(hh, rows, s, vx):
        m_prev = m_sc[hh, rows, :]
        m_new = jnp.maximum(m_prev, jnp.max(s, axis=-1, keepdims=True))
        alpha = jnp.exp2(m_prev - m_new)
        p = jnp.exp2(s - jnp.concatenate([m_new] * (s.shape[1] // LANES), axis=1))
        pv = jnp.dot(p.astype(BF16), vx, preferred_element_type=F32)
        l_sc[hh, rows, :] = alpha * l_sc[hh, rows, :] + pv[:, hd2:]
        acc_sc[hh, rows, :] = alpha * acc_sc[hh, rows, :] + pv[:, :hd2]
        m_sc[hh, rows, :] = m_new

    def values(ki, hh):
        v = v_ref[pl.ds(pl.multiple_of(ki * tile, tile), tile), hh * hd2:(hh + 1) * hd2]
        return jnp.concatenate([v, ones], axis=1)

    def step(ki, far):
        for hh in range(heads):
            vx = values(ki, hh)
            if far:
                bias = tbl_ref[hg * heads + hh, far_bucket] * LOG2E
            else:
                bias = bias_ref[hh, jnp.minimum(qi - ki, n_bias - 1)]
            for c in range(2):
                rows = slice(c * tile, (c + 1) * tile)
                s = s_sc[hh, rows, :] + bias
                s_sc[hh, rows, :] = scores(ki + 1, hh, c)
                update(hh, rows, s, vx)

    def diagonal_step():
        half = tile // 2
        for hh in range(heads):
            vx = values(qi, hh)
            for c in range(2):
                for r0 in range(0, tile, half):
                    rows = slice(c * tile + r0, c * tile + r0 + half)
                    nk = r0 + half
                    s = s_sc[hh, rows, 0:nk] + bias_ref[hh, 0, r0:r0 + half, 0:nk]
                    update(hh, rows, s, vx[0:nk, :])

    for hh in range(heads):
        for c in range(2):
            s_sc[hh, c * tile:(c + 1) * tile, :] = scores(0, hh, c)

    def far_body(ki, carry):
        step(ki, True)
        return carry

    def near_body(ki, carry):
        step(ki, False)
        return carry

    if far_bucket is None:
        n_far = 0
    else:
        n_far = jnp.maximum(qi - (n_bias - 1), 0)
        lax.fori_loop(0, n_far, far_body, 0)
    lax.fori_loop(n_far, qi, near_body, 0)
    diagonal_step()

    lam = (jnp.exp(jnp.sum(lq1[...].astype(F32) * lk1[...].astype(F32), axis=-1, keepdims=True))
           - jnp.exp(jnp.sum(lq2[...].astype(F32) * lk2[...].astype(F32), axis=-1, keepdims=True))
           + lambda_init)
    for hh in range(heads):
        o = acc_sc[hh] * (1.0 / l_sc[hh])
        a = o[0:tile, :] - lam * o[tile:2 * tile, :]
        y = _rms(a, sg_ref[...], SUBLN_EPS) * (1.0 - lambda_init)
        o_ref[:, hh * hd2:(hh + 1) * hd2] = y.astype(o_ref.dtype)


def _diff_attention(q, k, v, table, bias, far_bucket, lq1, lk1, lq2, lk2, subln_g, lambda_init, tile, heads=DIFF_HEADS):
    b, s, _ = q.shape
    n_bias = bias.shape[1]
    hd2 = 2 * HEAD_DIM
    hw = heads * hd2
    ng = N_HEADS // heads
    vec = lambda a: a.reshape(1, -1)
    small = lambda n: pl.BlockSpec((1, n), lambda h, bb, i: (0, 0))
    return pl.pallas_call(
        functools.partial(_diff_kernel, tile=tile, n_bias=n_bias, far_bucket=far_bucket, lambda_init=lambda_init,
                          heads=heads),
        grid=(ng, b, s // tile),
        in_specs=[pl.BlockSpec(memory_space=pltpu.SMEM),
                  pl.BlockSpec((None, tile, hw), lambda h, bb, i: (bb, i, h)),
                  pl.BlockSpec((None, s, hw), lambda h, bb, i: (bb, 0, h)),
                  pl.BlockSpec((None, s, hw), lambda h, bb, i: (bb, 0, h)),
                  pl.BlockSpec((heads, n_bias, tile, tile), lambda h, bb, i: (h, 0, 0, 0),
                               pipeline_mode=pl.Buffered(1)),
                  small(HEAD_DIM), small(HEAD_DIM), small(HEAD_DIM), small(HEAD_DIM), small(hd2)],
        out_specs=pl.BlockSpec((None, tile, hw), lambda h, bb, i: (bb, i, h)),
        out_shape=jax.ShapeDtypeStruct((b, s, N_HEADS * hd2), BF16),
        scratch_shapes=[pltpu.VMEM((heads, 2 * tile, hd2), BF16), pltpu.VMEM((heads, 2 * tile, tile), F32),
                        pltpu.VMEM((heads, 2 * tile, LANES), F32), pltpu.VMEM((heads, 2 * tile, LANES), F32),
                        pltpu.VMEM((heads, 2 * tile, hd2), F32)],
        compiler_params=_cparams(("arbitrary", "arbitrary", "arbitrary")),
        name="diff_attn",
    )(table, q, k, v, bias, vec(lq1), vec(lk1), vec(lq2), vec(lk2), vec(subln_g))


def kernel(x, rel_bias_table, norm_g, w_in_a, w_out_a, kv_norm_g, w_k_shared, w_v_shared, w_q_b,
           lam_q1, lam_k1, lam_q2, lam_k2, subln_g, w_out_b, w_up, conv_w, conv_b, w_down):
    b, s, d = x.shape
    depth = norm_g.shape[0]
    n_a = w_in_a.shape[0]
    bf = lambda a: a.astype(BF16)

    dil_bias = _bias_tiles(rel_bias_table, _dilated_bias_idx(), head_major=False, scale=LOG2E)
    diff_tile = min(DIFF_TILE, s)
    diff_idx, far_bucket = _diff_bias_idx(s, diff_tile)
    diff_bias = _bias_tiles(rel_bias_table, diff_idx, head_major=True, scale=LOG2E)

    h = x.reshape(b * s, d)
    k_sh = v_sh = q = None
    for layer in range(depth):
        g = norm_g[layer]
        if layer < n_a:
            qkvs = _proj_a(h, g[0], w_in_a[layer], b, s)
            outs = [_dilated_group(qkvs[gi], dil_bias, gi, dil) for gi, (_, dil) in enumerate(DIL_GROUPS)]
            h = _combine_out([o for o, _ in outs], [l for _, l in outs], w_out_a[layer], g[1], h, b, s)
            mixer_out = None
        else:
            j = layer - n_a
            lambda_init = 0.8 - 0.6 * math.exp(-0.3 * layer)
            if q is None:
                q, = _norm_matmul(h, [(g[0], w_q_b[j], Q_SCALE)])
            a = _diff_attention(q.reshape(b, s, -1), k_sh, v_sh, rel_bias_table, diff_bias, far_bucket, lam_q1[j],
                                lam_k1[j], lam_q2[j], lam_k2[j], subln_g[j], lambda_init, diff_tile)
            q = None
            mixer_out = (a.reshape(b * s, -1), w_out_b[j], g[1])
        h = _conv_ffn(h, g[2], bf(w_up[layer]), conv_w[layer], conv_b[layer], bf(w_down[layer]), g[3], s,
                      out_proj=mixer_out)
        if layer == n_a - 1:
            projs = [(kv_norm_g, w_k_shared, 1.0), (kv_norm_g, w_v_shared, 1.0)]
            if layer + 1 < depth:
                projs.append((norm_g[layer + 1, 0], w_q_b[0], Q_SCALE))
            res = _norm_matmul(h, projs)
            k_sh, v_sh = res[0].reshape(b, s, -1), res[1].reshape(b, s, -1)
            q = res[2] if len(res) > 2 else None
    return h.reshape(b, s, d)
```

```python
import functools
import math

import numpy as np
import jax
import jax.numpy as jnp
from jax import lax
from jax.experimental import pallas as pl
from jax.experimental.pallas import tpu as pltpu

F32 = jnp.float32
BF16 = jnp.bfloat16

HEAD_DIM = 64
N_HEADS = 8
DIL_GROUPS = ((128, 1), (512, 4), (2048, 16))
ATTN_BLOCK = 128
NUM_BUCKETS = 32
MAX_DISTANCE = 2048
CONV_WIDTH = 3
RMS_EPS = 1e-6
SUBLN_EPS = 1e-5

LANES = 128
SUBLANES = 8
ROW_TILE = 512
BIAS_ROWS = 32
VMEM_LIMIT = 56 * 1024 * 1024

NEG_INF = float("-inf")
LOG2E = math.log2(math.e)
DIFF_HEADS = 4
Q_SCALE = HEAD_DIM ** -0.5 * LOG2E


def _cparams(sem):
    return pltpu.CompilerParams(dimension_semantics=sem, vmem_limit_bytes=VMEM_LIMIT)


def _bucket_np(dist):
    n = np.maximum(dist, 0)
    max_exact = NUM_BUCKETS // 2

    def large(dtype):
        nf = np.maximum(n, 1).astype(dtype)
        v = np.log(nf / dtype(max_exact)) / dtype(math.log(MAX_DISTANCE / max_exact)) * dtype(NUM_BUCKETS - max_exact)
        return np.minimum(max_exact + v.astype(np.int32), NUM_BUCKETS - 1)

    l32, l64 = large(np.float32), large(np.float64)
    assert np.array_equal(l32, l64), "bucket boundary is rounding sensitive"
    return np.where(n < max_exact, n, l32).astype(np.int32)


def _bias_kernel(tbl_ref, idx_ref, o_ref, *, plan, scale):
    h = pl.program_id(0)
    for t, blocks in enumerate(plan):
        for rb, (buckets, has_mask) in enumerate(blocks):
            rows = slice(rb * BIAS_ROWS, (rb + 1) * BIAS_ROWS)
            shape = (BIAS_ROWS, idx_ref.shape[-1])
            if len(buckets) == 1 and not has_mask:
                o_ref[t, rows, :] = jnp.full(shape, tbl_ref[h, buckets[0]] * scale, F32)
                continue
            idx = idx_ref[t, rows, :]
            acc = jnp.full(shape, NEG_INF, F32)
            for k in buckets:
                acc = jnp.where(idx == k, tbl_ref[h, k] * scale, acc)
            o_ref[t, rows, :] = acc


def _bias_tiles(table, idx, head_major, scale=1.0):
    t, r, c = idx.shape
    nh = table.shape[0]
    plan = tuple(
        tuple((tuple(int(k) for k in np.unique(blk[blk >= 0])), bool((blk < 0).any()))
              for blk in idx[i].reshape(r // BIAS_ROWS, BIAS_ROWS * c))
        for i in range(t))
    if head_major:
        out_shape, out_block, out_map = (nh, t, r, c), (None, t, r, c), (lambda h: (h, 0, 0, 0))
    else:
        out_shape, out_block, out_map = (t, nh, r, c), (t, None, r, c), (lambda h: (0, h, 0, 0))
    return pl.pallas_call(
        functools.partial(_bias_kernel, plan=plan, scale=scale),
        grid=(nh,),
        in_specs=[pl.BlockSpec(memory_space=pltpu.SMEM), pl.BlockSpec((t, r, c), lambda h: (0, 0, 0))],
        out_specs=pl.BlockSpec(out_block, out_map),
        out_shape=jax.ShapeDtypeStruct(out_shape, F32),
        compiler_params=_cparams(("arbitrary",)),
        name="rel_bias_tiles",
    )(table, jnp.asarray(idx))


def _dilated_bias_idx():
    qi = np.arange(ATTN_BLOCK)[:, None]
    ki = np.arange(2 * ATTN_BLOCK)[None, :]
    dist_u = qi + ATTN_BLOCK - ki
    out = []
    for window, dil in DIL_GROUPS:
        band = (dist_u >= 0) & (dist_u <= window // dil)
        out.append(np.where(band, _bucket_np(dist_u * dil), -1))
    return np.stack(out).astype(np.int32)


def _diff_bias_idx(seq, tile):
    nq = seq // tile
    i = np.arange(tile)[:, None]
    j = np.arange(tile)[None, :]
    tiles = []
    for d in range(nq):
        dist = d * tile + i - j
        tiles.append(np.where(dist >= 0, _bucket_np(dist), -1).astype(np.int32))
    nd = nq
    while nd > 1 and np.array_equal(tiles[nd - 1], tiles[nd - 2]):
        nd -= 1
    assert all(np.array_equal(tiles[d], tiles[nd - 1]) for d in range(nd - 1, nq))
    far = np.unique(tiles[nd - 1])
    if nd >= 2 and far.size == 1 and far[0] >= 0:
        return np.stack(tiles[:nd - 1]), int(far[0])
    return np.stack(tiles[:nd]), None


def _rms(x, g, eps):
    return x * lax.rsqrt(jnp.mean(x * x, axis=-1, keepdims=True) + eps) * g


def _norm_matmul_kernel(x_ref, *refs, n_chunk, out_scales):
    n_proj = len(out_scales)
    x = x_ref[...]
    xs = x * lax.rsqrt(jnp.mean(x * x, axis=-1, keepdims=True) + RMS_EPS)
    for p in range(n_proj):
        g_ref, w_ref, o_ref = refs[2 * p], refs[2 * p + 1], refs[2 * n_proj + p]
        xn = (xs * g_ref[...]).astype(BF16)
        for j in range(o_ref.shape[-1] // n_chunk):
            sl = slice(j * n_chunk, (j + 1) * n_chunk)
            res = jnp.dot(xn, w_ref[:, sl], preferred_element_type=F32)
            if out_scales[p] != 1.0:
                res = res * out_scales[p]
            o_ref[:, sl] = res.astype(o_ref.dtype)


def _norm_matmul(x2d, projs, tm=ROW_TILE, n_chunk=512):
    m, d = x2d.shape
    ins, in_specs = [], []
    out_scales = tuple(float(sc) for _, _, sc in projs)
    projs = [(g, w) for g, w, _ in projs]
    for g, w in projs:
        ins += [g.reshape(1, d), w]
        in_specs += [pl.BlockSpec((1, d), lambda i: (0, 0)), pl.BlockSpec(w.shape, lambda i: (0, 0))]
    return pl.pallas_call(
        functools.partial(_norm_matmul_kernel, n_chunk=n_chunk, out_scales=out_scales),
        grid=(m // tm,),
        in_specs=[pl.BlockSpec((tm, d), lambda i: (i, 0))] + in_specs,
        out_specs=[pl.BlockSpec((tm, w.shape[1]), lambda i: (i, 0)) for _, w in projs],
        out_shape=[jax.ShapeDtypeStruct((m, w.shape[1]), BF16) for _, w in projs],
        compiler_params=_cparams(("arbitrary",)),
        name="norm_matmul",
    )(x2d, *ins)


def _proj_a_kernel(x_ref, g_ref, w_ref, *refs, n_chunk):
    out_refs, xn_sc = refs[:-1], refs[-1]
    tm = x_ref.shape[0]
    xn = _rms(x_ref[...], g_ref[...], RMS_EPS)
    nlb = xn.shape[1] // LANES
    for j in range(nlb):
        xn_sc[j] = xn[:, j * LANES:(j + 1) * LANES]
    gw = out_refs[0].shape[-1]
    for gi, (o_ref, (_, dil)) in enumerate(zip(out_refs, DIL_GROUPS)):
        per = tm // dil
        if dil == 1:
            xg = xn
        else:
            xg = jnp.concatenate(
                [jnp.concatenate([xn_sc[j, pl.ds(c, per, stride=dil), :] for c in range(dil)], axis=0)
                 for j in range(nlb)], axis=1)
        xg = xg.astype(BF16)
        for j in range(gw // n_chunk):
            res = jnp.dot(xg, w_ref[:, gi * gw + j * n_chunk:gi * gw + (j + 1) * n_chunk],
                          preferred_element_type=F32)
            if j == 0:
                res = res * Q_SCALE
            res = res.astype(o_ref.dtype)
            for c in range(dil):
                o_ref[c, :, j * n_chunk:(j + 1) * n_chunk] = res[c * per:(c + 1) * per, :]


def _proj_a(x2d, g, w_bf16, b, s, tm=ROW_TILE, n_chunk=512):
    m, d = x2d.shape
    gw = w_bf16.shape[1] // len(DIL_GROUPS)
    tps = s // tm
    assert all(tm % (16 * dil) == 0 for _, dil in DIL_GROUPS) and n_chunk == N_HEADS * HEAD_DIM
    return pl.pallas_call(
        functools.partial(_proj_a_kernel, n_chunk=n_chunk),
        grid=(b, tps),
        in_specs=[pl.BlockSpec((tm, d), lambda bb, i: (bb * tps + i, 0)),
                  pl.BlockSpec((1, d), lambda bb, i: (0, 0)),
                  pl.BlockSpec(w_bf16.shape, lambda bb, i: (0, 0))],
        out_specs=[pl.BlockSpec((None, dil, tm // dil, gw), lambda bb, i: (bb, 0, i, 0)) for _, dil in DIL_GROUPS],
        out_shape=[jax.ShapeDtypeStruct((b, dil, s // dil, gw), BF16) for _, dil in DIL_GROUPS],
        scratch_shapes=[pltpu.VMEM((d // LANES, tm, LANES), F32)],
        compiler_params=_cparams(("arbitrary", "arbitrary")),
        name="proj_dilated",
    )(x2d, g.reshape(1, d), w_bf16)


def _dil_kernel(q_ref, k_ref, v_ref, kp_ref, vp_ref, bias_ref, o_ref, lse_ref):
    blk = ATTN_BLOCK
    n_cls, tq = q_ref.shape[0], q_ref.shape[1]
    i = pl.program_id(2)
    lane = lax.broadcasted_iota(jnp.int32, (1, LANES), 1)
    col = lax.broadcasted_iota(jnp.int32, (blk, 2 * blk), 1)
    first_pen = jnp.where(jnp.logical_and(i == 0, col < blk), NEG_INF, 0.0).astype(F32)
    ones = jnp.ones((2 * blk, LANES), BF16)
    for cc in range(n_cls):
        for j in range(tq // blk):
            rows = slice(j * blk, (j + 1) * blk)
            qj = q_ref[cc, rows, :]
            if j == 0:
                kx = jnp.concatenate([kp_ref[cc], k_ref[cc, 0:blk, :]], axis=0)
                vx = jnp.concatenate([vp_ref[cc], v_ref[cc, 0:blk, :]], axis=0)
            else:
                kx = k_ref[cc, (j - 1) * blk:(j + 1) * blk, :]
                vx = v_ref[cc, (j - 1) * blk:(j + 1) * blk, :]
            lse_tile = jnp.zeros((blk, LANES), F32)
            for hp in range(N_HEADS // 2):
                cols = slice(hp * LANES, (hp + 1) * LANES)
                q2, k2 = qj[:, cols], kx[:, cols]
                v2 = jnp.concatenate([vx[:, cols], ones], axis=1)
                o_pair = None
                for e in range(2):
                    h = 2 * hp + e
                    sel = (lane >= HEAD_DIM) if e else (lane < HEAD_DIM)
                    qm = jnp.where(sel, q2, jnp.zeros_like(q2))
                    s = lax.dot_general(qm, k2, (((1,), (1,)), ((), ())), preferred_element_type=F32) + bias_ref[h]
                    if j == 0:
                        s = s + first_pen
                    m = jnp.max(s, axis=-1, keepdims=True)
                    p = jnp.exp2(s - m)
                    pv = jnp.dot(p.astype(BF16), v2, preferred_element_type=F32)
                    l = pv[:, LANES:]
                    o_e = pv[:, :LANES] * (1.0 / l)
                    o_pair = o_e if e == 0 else jnp.where(lane < HEAD_DIM, o_pair, o_e)
                    lse_tile = jnp.where(lane == h, m + jnp.log2(l), lse_tile)
                o_ref[cc, rows, cols] = o_pair.astype(o_ref.dtype)
            lse_ref[cc, rows, :] = lse_tile


def _dilated_group(qkv, bias, g, dil):
    b, r, n, _ = qkv.shape
    w = N_HEADS * HEAD_DIM
    assert r == dil and n % ATTN_BLOCK == 0
    tq = min(n, DIL_ROWS)
    n_cls = min(dil, DIL_ROWS // tq)
    sub = tq // ATTN_BLOCK

    def cur(which):
        return pl.BlockSpec((None, n_cls, tq, w), lambda bb, c, i: (bb, c, i, which))

    def prev(which):
        return pl.BlockSpec((None, n_cls, ATTN_BLOCK, w),
                            lambda bb, c, i: (bb, c, jnp.maximum(i * sub - 1, 0), which))

    return pl.pallas_call(
        _dil_kernel,
        grid=(b, dil // n_cls, n // tq),
        in_specs=[cur(0), cur(1), cur(2), prev(1), prev(2),
                  pl.BlockSpec((None, N_HEADS, ATTN_BLOCK, 2 * ATTN_BLOCK), lambda bb, c, i: (g, 0, 0, 0))],
        out_specs=[pl.BlockSpec((None, n_cls, tq, w), lambda bb, c, i: (bb, c, i, 0)),
                   pl.BlockSpec((None, n_cls, tq, LANES), lambda bb, c, i: (bb, c, i, 0))],
        out_shape=[jax.ShapeDtypeStruct((b, dil, n, w), BF16),
                   jax.ShapeDtypeStruct((b, dil, n, LANES), F32)],
        compiler_params=_cparams(("arbitrary", "arbitrary", "arbitrary")),
        name=f"dilated_attn_g{g}",
    )(qkv, qkv, qkv, qkv, qkv, bias)


def _combine_out_kernel(o0, o1, o2, l0, l1, l2, e_ref, w_ref, g_ref, h_ref, out_ref, o_sc, l_sc):
    tm = h_ref.shape[0]

    def token_order(o_ref, l_ref, slot):
        dil = o_ref.shape[0]
        if dil == 1:
            return o_ref[0].astype(F32), l_ref[0]
        per = tm // dil
        nlb = o_ref.shape[-1] // LANES
        for c in range(dil):
            oc = o_ref[c].astype(F32)
            for j in range(nlb):
                o_sc[slot, j, pl.ds(c, per, stride=dil), :] = oc[:, j * LANES:(j + 1) * LANES]
            l_sc[slot, pl.ds(c, per, stride=dil), :] = l_ref[c]
        return jnp.concatenate([o_sc[slot, j] for j in range(nlb)], axis=1), l_sc[slot]

    pairs = [token_order(o_ref, l_ref, slot) for slot, (o_ref, l_ref) in enumerate(((o0, l0), (o1, l1), (o2, l2)))]
    ls = [l for _, l in pairs]
    mx = jnp.maximum(jnp.maximum(ls[0], ls[1]), ls[2])
    ws = [jnp.exp2(l - mx) for l in ls]
    inv = 1.0 / (ws[0] + ws[1] + ws[2])
    e = e_ref[...]

    def expand(wg):
        rem = wg * inv
        ax = None
        for _ in range(3):
            piece = rem.astype(BF16)
            rem = rem - piece.astype(F32)
            part = jnp.dot(piece, e, preferred_element_type=F32)
            ax = part if ax is None else ax + part
        return ax

    o_last = pairs[-1][0]
    a = o_last
    for (og, _), wg in zip(pairs[:-1], ws[:-1]):
        a = a + expand(wg) * (og - o_last)
    y = jnp.dot(a.astype(BF16), w_ref[...], preferred_element_type=F32)
    out_ref[...] = h_ref[...] + _rms(y, g_ref[...], RMS_EPS)


def _combine_out(os_, lses, w_bf16, g, h2d, b, s, tm=ROW_TILE):
    m, d = h2d.shape
    w = os_[0].shape[-1]
    tps = s // tm
    expand = np.zeros((LANES, w), np.float32)
    for hh in range(N_HEADS):
        expand[hh, hh * HEAD_DIM:(hh + 1) * HEAD_DIM] = 1.0
    grp = lambda a: pl.BlockSpec((None, a.shape[1], tm // a.shape[1], a.shape[3]), lambda bb, i: (bb, 0, i, 0))
    row = pl.BlockSpec((tm, d), lambda bb, i: (bb * tps + i, 0))
    full = lambda shp: pl.BlockSpec(shp, lambda bb, i: (0, 0))
    return pl.pallas_call(
        _combine_out_kernel,
        grid=(b, tps),
        in_specs=[grp(a) for a in os_] + [grp(a) for a in lses]
                 + [full((LANES, w)), full((w, d)), full((1, d)), row],
        out_specs=row,
        out_shape=jax.ShapeDtypeStruct((m, d), F32),
        scratch_shapes=[pltpu.VMEM((3, w // LANES, tm, LANES), F32), pltpu.VMEM((3, tm, LANES), F32)],
        compiler_params=_cparams(("arbitrary", "arbitrary")),
        name="combine_out_proj",
    )(*os_, *lses, jnp.asarray(expand, BF16), w_bf16, g.reshape(1, d), h2d)


def _ffn_kernel(*refs, tiles_per_seq, f_chunk, fuse_out_proj):
    if fuse_out_proj:
        a_ref, wo_ref, g_mix_ref, refs = refs[0], refs[1], refs[2], refs[3:]
    (h_ref, g_in_ref, wup_ref, cw_ref, cb_ref, wdn_ref, g_out_ref, out_ref,
     h_sc, xn_sc, halo_sc, acc_sc, act_sc) = refs[:13]
    u_bufs = refs[13:]
    tm = h_ref.shape[0]
    d_ff = wdn_ref.shape[0]
    nc = d_ff // f_chunk
    i = pl.program_id(0)
    if fuse_out_proj:
        mix = jnp.dot(a_ref[...], wo_ref[...], preferred_element_type=F32)
        h_sc[...] = h_ref[...] + _rms(mix, g_mix_ref[...], RMS_EPS)
    else:
        h_sc[...] = h_ref[...]
    xn_sc[...] = _rms(h_sc[...], g_in_ref[...], RMS_EPS).astype(BF16)
    seq_start = (i % tiles_per_seq) == 0
    inv_sqrt2 = 1.0 / math.sqrt(2.0)

    def cols(c, j):
        return slice(j * d_ff + c * f_chunk, j * d_ff + (c + 1) * f_chunk)

    def up(c, u_sc):
        for j in range(2):
            u_sc[j, SUBLANES:tm + SUBLANES, :] = jnp.dot(xn_sc[...], wup_ref[:, cols(c, j)],
                                                         preferred_element_type=F32)

    def conv(u_sc, c, j):
        cw = cw_ref[:, cols(c, j)]
        out = cb_ref[:, cols(c, j)]
        for t in range(CONV_WIDTH):
            off = SUBLANES - (CONV_WIDTH - 1 - t)
            out = out + cw[t:t + 1, :] * u_sc[j, off:tm + off, :]
        return out

    def down(c, u_sc):
        u_sc[:, 0:SUBLANES, :] = jnp.where(seq_start, 0.0, halo_sc[c])
        halo_sc[c] = u_sc[:, tm:tm + SUBLANES, :]
        gate = conv(u_sc, c, 0)
        val = conv(u_sc, c, 1)
        act = (0.5 * gate * (1.0 + lax.erf(gate * inv_sqrt2)) * val).astype(BF16)
        g0 = c - c % FFN_DOWN_GROUP
        act_sc[:, (c - g0) * f_chunk:(c - g0 + 1) * f_chunk] = act
        if c == nc - 1 or c % FFN_DOWN_GROUP == FFN_DOWN_GROUP - 1:
            k = (c + 1 - g0) * f_chunk
            part = jnp.dot(act_sc[:, 0:k], wdn_ref[g0 * f_chunk:(c + 1) * f_chunk, :], preferred_element_type=F32)
            if g0 == 0:
                acc_sc[...] = part
            else:
                acc_sc[...] += part

    for c in range(min(FFN_AHEAD, nc)):
        up(c, u_bufs[c])
    for c in range(nc):
        if c + FFN_AHEAD < nc:
            up(c + FFN_AHEAD, u_bufs[(c + FFN_AHEAD) % (FFN_AHEAD + 1)])
        down(c, u_bufs[c % (FFN_AHEAD + 1)])
    out_ref[...] = h_sc[...] + _rms(acc_sc[...], g_out_ref[...], RMS_EPS)


def _conv_ffn(h2d, g_in, w_up_bf16, conv_w, conv_b, w_down_bf16, g_out, seq, out_proj=None, tm=ROW_TILE, f_chunk=256):
    m, d = h2d.shape
    f2 = w_up_bf16.shape[1]
    d_ff = f2 // 2
    assert seq % tm == 0 and d_ff % f_chunk == 0 and CONV_WIDTH - 1 <= SUBLANES
    nc = d_ff // f_chunk
    full = lambda shp: pl.BlockSpec(shp, lambda i: (0, 0))
    row = lambda width: pl.BlockSpec((tm, width), lambda i: (i, 0))
    ins = [h2d, g_in.reshape(1, d), w_up_bf16, conv_w, conv_b.reshape(1, f2), w_down_bf16, g_out.reshape(1, d)]
    in_specs = [row(d), full((1, d)), full((d, f2)), full((CONV_WIDTH, f2)), full((1, f2)), full((d_ff, d)),
                full((1, d))]
    if out_proj is not None:
        a2d, w_out_bf16, g_mix = out_proj
        ins = [a2d, w_out_bf16, g_mix.reshape(1, d)] + ins
        in_specs = [row(a2d.shape[1]), full(w_out_bf16.shape), full((1, d))] + in_specs
    return pl.pallas_call(
        functools.partial(_ffn_kernel, tiles_per_seq=seq // tm, f_chunk=f_chunk, fuse_out_proj=out_proj is not None),
        grid=(m // tm,),
        in_specs=in_specs,
        out_specs=row(d),
        out_shape=jax.ShapeDtypeStruct((m, d), F32),
        scratch_shapes=[pltpu.VMEM((tm, d), F32), pltpu.VMEM((tm, d), BF16),
                        pltpu.VMEM((nc, 2, SUBLANES, f_chunk), F32), pltpu.VMEM((tm, d), F32),
                        pltpu.VMEM((tm, FFN_DOWN_GROUP * f_chunk), BF16)]
                       + [pltpu.VMEM((2, tm + SUBLANES, f_chunk), F32)] * (FFN_AHEAD + 1),
        compiler_params=_cparams(("arbitrary",)),
        name="conv_ffn",
    )(*ins)


def _diff_kernel(tbl_ref, q_ref, k_ref, v_ref, bias_ref, lq1, lk1, lq2, lk2, sg_ref, o_ref,
                 qs_sc, s_sc, m_sc, l_sc, acc_sc, *, tile, n_bias, far_bucket, lambda_init, heads):
    hg = pl.program_id(0)
    qi = pl.program_id(2)
    hd2 = 2 * HEAD_DIM
    lane = lax.broadcasted_iota(jnp.int32, (1, LANES), 1)
    for hh in range(heads):
        q = q_ref[:, hh * hd2:(hh + 1) * hd2]
        zero = jnp.zeros_like(q)
        qs_sc[hh, 0:tile, :] = jnp.where(lane < HEAD_DIM, q, zero)
        qs_sc[hh, tile:2 * tile, :] = jnp.where(lane >= HEAD_DIM, q, zero)
    m_sc[...] = jnp.full(m_sc.shape, NEG_INF, F32)
    l_sc[...] = jnp.zeros(l_sc.shape, F32)
    acc_sc[...] = jnp.zeros(acc_sc.shape, F32)
    ones = jnp.ones((tile, LANES), BF16)

    def scores(ki, hh, c):
        k = k_ref[pl.ds(pl.multiple_of(ki * tile, tile), tile), hh * hd2:(hh + 1) * hd2]
        return lax.dot_general(qs_sc[hh, c * tile:(c + 1) * tile, :], k, (((1,), (1,)), ((), ())),
                               preferred_element_type=F32)

    def update(hh, rows, s, vx):
        m_prev = m_sc[hh, rows, :]
        m_new = jnp.maximum(m_prev, jnp.max(s, axis=-1, keepdims=True))
        alpha = jnp.exp2(m_prev - m_new)
        p = jnp.exp2(s - jnp.concatenate([m_new] * (s.shape[1] // LANES), axis=1))
        pv = jnp.dot(p.astype(BF16), vx, preferred_element_type=F32)
        l_sc[hh, rows, :] = alpha * l_sc[hh, rows, :] + pv[:, hd2:]
        acc_sc[hh, rows, :] = alpha * acc_sc[hh, rows, :] + pv[:, :hd2]
        m_sc[hh, rows, :] = m_new

    def values(ki, hh):
        v = v_ref[pl.ds(pl.multiple_of(ki * tile, tile), tile), hh * hd2:(hh + 1) * hd2]
        return jnp.concatenate([v, ones], axis=1)

    def step(ki, far):
        for hh in range(heads):
            vx = values(ki, hh)
            if far:
                bias = tbl_ref[hg * heads + hh, far_bucket] * LOG2E
            else:
                bias = bias_ref[hh, jnp.minimum(qi - ki, n_bias - 1)]
            for c in range(2):
                rows = slice(c * tile, (c + 1) * tile)
                s = s_sc[hh, rows, :] + bias
                s_sc[hh, rows, :] = scores(ki + 1, hh, c)
                update(hh, rows, s, vx)

    def diagonal_step():
        half = tile // 2
        for hh in range(heads):
            vx = values(qi, hh)
            for c in range(2):
                for r0 in range(0, tile, half):
                    rows = slice(c * tile + r0, c * tile + r0 + half)
                    nk = r0 + half
                    s = s_sc[hh, rows, 0:nk] + bias_ref[hh, 0, r0:r0 + half, 0:nk]
                    update(hh, rows, s, vx[0:nk, :])

    for hh in range(heads):
        for c in range(2):
            s_sc[hh, c * tile:(c + 1) * tile, :] = scores(0, hh, c)

    def far_body(ki, carry):
        step(ki, True)
        return carry

    def near_body(ki, carry):
        step(ki, False)
        return carry

    if far_bucket is None:
        n_far = 0
    else:
        n_far = jnp.maximum(qi - (n_bias - 1), 0)
        lax.fori_loop(0, n_far, far_body, 0)
    lax.fori_loop(n_far, qi, near_body, 0)
    diagonal_step()

    lam = (jnp.exp(jnp.sum(lq1[...].astype(F32) * lk1[...].astype(F32), axis=-1, keepdims=True))
           - jnp.exp(jnp.sum(lq2[...].astype(F32) * lk2[...].astype(F32), axis=-1, keepdims=True))
           + lambda_init)
    for hh in range(heads):
        o = acc_sc[hh] * (1.0 / l_sc[hh])
        a = o[0:tile, :] - lam * o[tile:2 * tile, :]
        y = _rms(a, sg_ref[...], SUBLN_EPS) * (1.0 - lambda_init)
        o_ref[:, hh * hd2:(hh + 1) * hd2] = y.astype(o_ref.dtype)


def _diff_attention(q, k, v, table, bias, far_bucket, lq1, lk1, lq2, lk2, subln_g, lambda_init, tile, heads=DIFF_HEADS):
    b, s, _ = q.shape
    n_bias = bias.shape[1]
    hd2 = 2 * HEAD_DIM
    hw = heads * hd2
    ng = N_HEADS // heads
    vec = lambda a: a.reshape(1, -1)
    small = lambda n: pl.BlockSpec((1, n), lambda h, bb, i: (0, 0))
    return pl.pallas_call(
        functools.partial(_diff_kernel, tile=tile, n_bias=n_bias, far_bucket=far_bucket, lambda_init=lambda_init,
                          heads=heads),
        grid=(ng, b, s // tile),
        in_specs=[pl.BlockSpec(memory_space=pltpu.SMEM),
                  pl.BlockSpec((None, tile, hw), lambda h, bb, i: (bb, i, h)),
                  pl.BlockSpec((None, s, hw), lambda h, bb, i: (bb, 0, h)),
                  pl.BlockSpec((None, s, hw), lambda h, bb, i: (bb, 0, h)),
                  pl.BlockSpec((heads, n_bias, tile, tile), lambda h, bb, i: (h, 0, 0, 0),
                               pipeline_mode=pl.Buffered(1)),
                  small(HEAD_DIM), small(HEAD_DIM), small(HEAD_DIM), small(HEAD_DIM), small(hd2)],
        out_specs=pl.BlockSpec((None, tile, hw), lambda h, bb, i: (bb, i, h)),
        out_shape=jax.ShapeDtypeStruct((b, s, N_HEADS * hd2), BF16),
        scratch_shapes=[pltpu.VMEM((heads, 2 * tile, hd2), BF16), pltpu.VMEM((heads, 2 * tile, tile), F32),
                        pltpu.VMEM((heads, 2 * tile, LANES), F32), pltpu.VMEM((heads, 2 * tile, LANES), F32),
                        pltpu.VMEM((heads, 2 * tile, hd2), F32)],
        compiler_params=_cparams(("arbitrary", "arbitrary", "arbitrary")),
        name="diff_attn",
    )(table, q, k, v, bias, vec(lq1), vec(lk1), vec(lq2), vec(lk2), vec(subln_g))


DIFF_TILE = 512
DIL_ROWS = 2048
FFN_AHEAD = 8
FFN_DOWN_GROUP = 4


def kernel(x, rel_bias_table, norm_g, w_in_a, w_out_a, kv_norm_g, w_k_shared, w_v_shared, w_q_b,
           lam_q1, lam_k1, lam_q2, lam_k2, subln_g, w_out_b, w_up, conv_w, conv_b, w_down):
    b, s, d = x.shape
    depth = norm_g.shape[0]
    n_a = w_in_a.shape[0]
    bf = lambda a: a.astype(BF16)

    dil_bias = _bias_tiles(rel_bias_table, _dilated_bias_idx(), head_major=False, scale=LOG2E)
    diff_tile = min(DIFF_TILE, s)
    diff_idx, far_bucket = _diff_bias_idx(s, diff_tile)
    diff_bias = _bias_tiles(rel_bias_table, diff_idx, head_major=True, scale=LOG2E)

    h = x.reshape(b * s, d)
    k_sh = v_sh = q = None
    for layer in range(depth):
        g = norm_g[layer]
        if layer < n_a:
            qkvs = _proj_a(h, g[0], bf(w_in_a[layer]), b, s)
            outs = [_dilated_group(qkvs[gi], dil_bias, gi, dil) for gi, (_, dil) in enumerate(DIL_GROUPS)]
            h = _combine_out([o for o, _ in outs], [l for _, l in outs], bf(w_out_a[layer]), g[1], h, b, s)
            mixer_out = None
        else:
            j = layer - n_a
            lambda_init = 0.8 - 0.6 * math.exp(-0.3 * layer)
            if q is None:
                q, = _norm_matmul(h, [(g[0], bf(w_q_b[j]), Q_SCALE)])
            a = _diff_attention(q.reshape(b, s, -1), k_sh, v_sh, rel_bias_table, diff_bias, far_bucket, lam_q1[j],
                                lam_k1[j], lam_q2[j], lam_k2[j], subln_g[j], lambda_init, diff_tile)
            q = None
            mixer_out = (a.reshape(b * s, -1), bf(w_out_b[j]), g[1])
        h = _conv_ffn(h, g[2], bf(w_up[layer]), conv_w[layer], conv_b[layer], bf(w_down[layer]), g[3], s,
                      out_proj=mixer_out)
        if layer == n_a - 1:
            projs = [(kv_norm_g, bf(w_k_shared), 1.0), (kv_norm_g, bf(w_v_shared), 1.0)]
            if layer + 1 < depth:
                projs.append((norm_g[layer + 1, 0], bf(w_q_b[0]), Q_SCALE))
            res = _norm_matmul(h, projs)
            k_sh, v_sh = res[0].reshape(b, s, -1), res[1].reshape(b, s, -1)
            q = res[2] if len(res) > 2 else None
    return h.reshape(b, s, d)
```

```python
import functools
import math

import numpy as np
import jax
import jax.numpy as jnp
from jax import lax
from jax.experimental import pallas as pl
from jax.experimental.pallas import tpu as pltpu

F32 = jnp.float32
BF16 = jnp.bfloat16

HEAD_DIM = 64
N_HEADS = 8
DIL_GROUPS = ((128, 1), (512, 4), (2048, 16))
ATTN_BLOCK = 128
NUM_BUCKETS = 32
MAX_DISTANCE = 2048
CONV_WIDTH = 3
RMS_EPS = 1e-6
SUBLN_EPS = 1e-5

LANES = 128
SUBLANES = 8
ROW_TILE = 512
BIAS_ROWS = 32
VMEM_LIMIT = 56 * 1024 * 1024

NEG_INF = float("-inf")
LOG2E = math.log2(math.e)
DIFF_HEADS = 4
Q_SCALE = HEAD_DIM ** -0.5 * LOG2E


def _cparams(sem):
    return pltpu.CompilerParams(dimension_semantics=sem, vmem_limit_bytes=VMEM_LIMIT)


def _bucket_np(dist):
    n = np.maximum(dist, 0)
    max_exact = NUM_BUCKETS // 2

    def large(dtype):
        nf = np.maximum(n, 1).astype(dtype)
        v = np.log(nf / dtype(max_exact)) / dtype(math.log(MAX_DISTANCE / max_exact)) * dtype(NUM_BUCKETS - max_exact)
        return np.minimum(max_exact + v.astype(np.int32), NUM_BUCKETS - 1)

    l32, l64 = large(np.float32), large(np.float64)
    assert np.array_equal(l32, l64), "bucket boundary is rounding sensitive"
    return np.where(n < max_exact, n, l32).astype(np.int32)


def _bias_kernel(tbl_ref, idx_ref, o_ref, *, plan, scale):
    h = pl.program_id(0)
    for t, blocks in enumerate(plan):
        for rb, (buckets, has_mask) in enumerate(blocks):
            rows = slice(rb * BIAS_ROWS, (rb + 1) * BIAS_ROWS)
            shape = (BIAS_ROWS, idx_ref.shape[-1])
            if len(buckets) == 1 and not has_mask:
                o_ref[t, rows, :] = jnp.full(shape, tbl_ref[h, buckets[0]] * scale, F32)
                continue
            idx = idx_ref[t, rows, :]
            acc = jnp.full(shape, NEG_INF, F32)
            for k in buckets:
                acc = jnp.where(idx == k, tbl_ref[h, k] * scale, acc)
            o_ref[t, rows, :] = acc


def _bias_tiles(table, idx, head_major, scale=1.0):
    t, r, c = idx.shape
    nh = table.shape[0]
    plan = tuple(
        tuple((tuple(int(k) for k in np.unique(blk[blk >= 0])), bool((blk < 0).any()))
              for blk in idx[i].reshape(r // BIAS_ROWS, BIAS_ROWS * c))
        for i in range(t))
    if head_major:
        out_shape, out_block, out_map = (nh, t, r, c), (None, t, r, c), (lambda h: (h, 0, 0, 0))
    else:
        out_shape, out_block, out_map = (t, nh, r, c), (t, None, r, c), (lambda h: (0, h, 0, 0))
    return pl.pallas_call(
        functools.partial(_bias_kernel, plan=plan, scale=scale),
        grid=(nh,),
        in_specs=[pl.BlockSpec(memory_space=pltpu.SMEM), pl.BlockSpec((t, r, c), lambda h: (0, 0, 0))],
        out_specs=pl.BlockSpec(out_block, out_map),
        out_shape=jax.ShapeDtypeStruct(out_shape, F32),
        compiler_params=_cparams(("arbitrary",)),
        name="rel_bias_tiles",
    )(table, jnp.asarray(idx))


def _dilated_bias_idx():
    qi = np.arange(ATTN_BLOCK)[:, None]
    ki = np.arange(2 * ATTN_BLOCK)[None, :]
    dist_u = qi + ATTN_BLOCK - ki
    out = []
    for window, dil in DIL_GROUPS:
        band = (dist_u >= 0) & (dist_u <= window // dil)
        out.append(np.where(band, _bucket_np(dist_u * dil), -1))
    return np.stack(out).astype(np.int32)


def _diff_bias_idx(seq, tile):
    nq = seq // tile
    i = np.arange(tile)[:, None]
    j = np.arange(tile)[None, :]
    tiles = []
    for d in range(nq):
        dist = d * tile + i - j
        tiles.append(np.where(dist >= 0, _bucket_np(dist), -1).astype(np.int32))
    nd = nq
    while nd > 1 and np.array_equal(tiles[nd - 1], tiles[nd - 2]):
        nd -= 1
    assert all(np.array_equal(tiles[d], tiles[nd - 1]) for d in range(nd - 1, nq))
    far = np.unique(tiles[nd - 1])
    if nd >= 2 and far.size == 1 and far[0] >= 0:
        return np.stack(tiles[:nd - 1]), int(far[0])
    return np.stack(tiles[:nd]), None


def _rms(x, g, eps):
    return x * lax.rsqrt(jnp.mean(x * x, axis=-1, keepdims=True) + eps) * g


def _norm_matmul_kernel(x_ref, *refs, n_chunk, out_scales):
    n_proj = len(out_scales)
    x = x_ref[...]
    xs = x * lax.rsqrt(jnp.mean(x * x, axis=-1, keepdims=True) + RMS_EPS)
    for p in range(n_proj):
        g_ref, w_ref, o_ref = refs[2 * p], refs[2 * p + 1], refs[2 * n_proj + p]
        xn = (xs * g_ref[...]).astype(BF16)
        for j in range(o_ref.shape[-1] // n_chunk):
            sl = slice(j * n_chunk, (j + 1) * n_chunk)
            res = jnp.dot(xn, w_ref[:, sl], preferred_element_type=F32)
            if out_scales[p] != 1.0:
                res = res * out_scales[p]
            o_ref[:, sl] = res.astype(o_ref.dtype)


def _norm_matmul(x2d, projs, tm=ROW_TILE, n_chunk=512):
    m, d = x2d.shape
    ins, in_specs = [], []
    out_scales = tuple(float(sc) for _, _, sc in projs)
    projs = [(g, w) for g, w, _ in projs]
    for g, w in projs:
        ins += [g.reshape(1, d), w]
        in_specs += [pl.BlockSpec((1, d), lambda i: (0, 0)), pl.BlockSpec(w.shape, lambda i: (0, 0))]
    return pl.pallas_call(
        functools.partial(_norm_matmul_kernel, n_chunk=n_chunk, out_scales=out_scales),
        grid=(m // tm,),
        in_specs=[pl.BlockSpec((tm, d), lambda i: (i, 0))] + in_specs,
        out_specs=[pl.BlockSpec((tm, w.shape[1]), lambda i: (i, 0)) for _, w in projs],
        out_shape=[jax.ShapeDtypeStruct((m, w.shape[1]), BF16) for _, w in projs],
        compiler_params=_cparams(("arbitrary",)),
        name="norm_matmul",
    )(x2d, *ins)


def _proj_a_kernel(x_ref, g_ref, w_ref, *refs, n_chunk):
    out_refs, xn_sc = refs[:-1], refs[-1]
    tm = x_ref.shape[0]
    xn = _rms(x_ref[...], g_ref[...], RMS_EPS)
    nlb = xn.shape[1] // LANES
    for j in range(nlb):
        xn_sc[j] = xn[:, j * LANES:(j + 1) * LANES]
    gw = out_refs[0].shape[-1]
    for gi, (o_ref, (_, dil)) in enumerate(zip(out_refs, DIL_GROUPS)):
        per = tm // dil
        if dil == 1:
            xg = xn
        else:
            xg = jnp.concatenate(
                [jnp.concatenate([xn_sc[j, pl.ds(c, per, stride=dil), :] for c in range(dil)], axis=0)
                 for j in range(nlb)], axis=1)
        xg = xg.astype(BF16)
        for j in range(gw // n_chunk):
            res = jnp.dot(xg, w_ref[:, gi * gw + j * n_chunk:gi * gw + (j + 1) * n_chunk],
                          preferred_element_type=F32)
            if j == 0:
                res = res * Q_SCALE
            res = res.astype(o_ref.dtype)
            for c in range(dil):
                o_ref[c, :, j * n_chunk:(j + 1) * n_chunk] = res[c * per:(c + 1) * per, :]


def _proj_a(x2d, g, w_bf16, b, s, tm=ROW_TILE, n_chunk=512):
    m, d = x2d.shape
    gw = w_bf16.shape[1] // len(DIL_GROUPS)
    tps = s // tm
    assert all(tm % (16 * dil) == 0 for _, dil in DIL_GROUPS) and n_chunk == N_HEADS * HEAD_DIM
    return pl.pallas_call(
        functools.partial(_proj_a_kernel, n_chunk=n_chunk),
        grid=(b, tps),
        in_specs=[pl.BlockSpec((tm, d), lambda bb, i: (bb * tps + i, 0)),
                  pl.BlockSpec((1, d), lambda bb, i: (0, 0)),
                  pl.BlockSpec(w_bf16.shape, lambda bb, i: (0, 0))],
        out_specs=[pl.BlockSpec((None, dil, tm // dil, gw), lambda bb, i: (bb, 0, i, 0)) for _, dil in DIL_GROUPS],
        out_shape=[jax.ShapeDtypeStruct((b, dil, s // dil, gw), BF16) for _, dil in DIL_GROUPS],
        scratch_shapes=[pltpu.VMEM((d // LANES, tm, LANES), F32)],
        compiler_params=_cparams(("arbitrary", "arbitrary")),
        name="proj_dilated",
    )(x2d, g.reshape(1, d), w_bf16)


def _dil_kernel(q_ref, k_ref, v_ref, kp_ref, vp_ref, bias_ref, o_ref, lse_ref):
    blk = ATTN_BLOCK
    n_cls, tq = q_ref.shape[0], q_ref.shape[1]
    i = pl.program_id(2)
    lane = lax.broadcasted_iota(jnp.int32, (1, LANES), 1)
    col = lax.broadcasted_iota(jnp.int32, (blk, 2 * blk), 1)
    first_pen = jnp.where(jnp.logical_and(i == 0, col < blk), NEG_INF, 0.0).astype(F32)
    ones = jnp.ones((2 * blk, LANES), BF16)
    for cc in range(n_cls):
        for j in range(tq // blk):
            rows = slice(j * blk, (j + 1) * blk)
            qj = q_ref[cc, rows, :]
            if j == 0:
                kx = jnp.concatenate([kp_ref[cc], k_ref[cc, 0:blk, :]], axis=0)
                vx = jnp.concatenate([vp_ref[cc], v_ref[cc, 0:blk, :]], axis=0)
            else:
                kx = k_ref[cc, (j - 1) * blk:(j + 1) * blk, :]
                vx = v_ref[cc, (j - 1) * blk:(j + 1) * blk, :]
            lse_tile = jnp.zeros((blk, LANES), F32)
            for hp in range(N_HEADS // 2):
                cols = slice(hp * LANES, (hp + 1) * LANES)
                q2, k2 = qj[:, cols], kx[:, cols]
                v2 = jnp.concatenate([vx[:, cols], ones], axis=1)
                o_pair = None
                for e in range(2):
                    h = 2 * hp + e
                    sel = (lane >= HEAD_DIM) if e else (lane < HEAD_DIM)
                    qm = jnp.where(sel, q2, jnp.zeros_like(q2))
                    s = lax.dot_general(qm, k2, (((1,), (1,)), ((), ())), preferred_element_type=F32) + bias_ref[h]
                    if j == 0:
                        s = s + first_pen
                    m = jnp.max(s, axis=-1, keepdims=True)
                    p = jnp.exp2(s - m)
                    pv = jnp.dot(p.astype(BF16), v2, preferred_element_type=F32)
                    l = pv[:, LANES:]
                    o_e = pv[:, :LANES] * (1.0 / l)
                    o_pair = o_e if e == 0 else jnp.where(lane < HEAD_DIM, o_pair, o_e)
                    lse_tile = jnp.where(lane == h, m + jnp.log2(l), lse_tile)
                o_ref[cc, rows, cols] = o_pair.astype(o_ref.dtype)
            lse_ref[cc, rows, :] = lse_tile


def _dilated_group(qkv, bias, g, dil):
    b, r, n, _ = qkv.shape
    w = N_HEADS * HEAD_DIM
    assert r == dil and n % ATTN_BLOCK == 0
    tq = min(n, DIL_ROWS)
    n_cls = min(dil, DIL_ROWS // tq)
    sub = tq // ATTN_BLOCK

    def cur(which):
        return pl.BlockSpec((None, n_cls, tq, w), lambda bb, c, i: (bb, c, i, which))

    def prev(which):
        return pl.BlockSpec((None, n_cls, ATTN_BLOCK, w),
                            lambda bb, c, i: (bb, c, jnp.maximum(i * sub - 1, 0), which))

    return pl.pallas_call(
        _dil_kernel,
        grid=(b, dil // n_cls, n // tq),
        in_specs=[cur(0), cur(1), cur(2), prev(1), prev(2),
                  pl.BlockSpec((None, N_HEADS, ATTN_BLOCK, 2 * ATTN_BLOCK), lambda bb, c, i: (g, 0, 0, 0))],
        out_specs=[pl.BlockSpec((None, n_cls, tq, w), lambda bb, c, i: (bb, c, i, 0)),
                   pl.BlockSpec((None, n_cls, tq, LANES), lambda bb, c, i: (bb, c, i, 0))],
        out_shape=[jax.ShapeDtypeStruct((b, dil, n, w), BF16),
                   jax.ShapeDtypeStruct((b, dil, n, LANES), F32)],
        compiler_params=_cparams(("arbitrary", "arbitrary", "arbitrary")),
        name=f"dilated_attn_g{g}",
    )(qkv, qkv, qkv, qkv, qkv, bias)


def _combine_out_kernel(o0, o1, o2, l0, l1, l2, e_ref, w_ref, g_ref, h_ref, out_ref, o_sc, l_sc):
    tm = h_ref.shape[0]

    def token_order(o_ref, l_ref, slot):
        dil = o_ref.shape[0]
        if dil == 1:
            return o_ref[0].astype(F32), l_ref[0]
        per = tm // dil
        nlb = o_ref.shape[-1] // LANES
        for c in range(dil):
            oc = o_ref[c].astype(F32)
            for j in range(nlb):
                o_sc[slot, j, pl.ds(c, per, stride=dil), :] = oc[:, j * LANES:(j + 1) * LANES]
            l_sc[slot, pl.ds(c, per, stride=dil), :] = l_ref[c]
        return jnp.concatenate([o_sc[slot, j] for j in range(nlb)], axis=1), l_sc[slot]

    pairs = [token_order(o_ref, l_ref, slot) for slot, (o_ref, l_ref) in enumerate(((o0, l0), (o1, l1), (o2, l2)))]
    ls = [l for _, l in pairs]
    mx = jnp.maximum(jnp.maximum(ls[0], ls[1]), ls[2])
    ws = [jnp.exp2(l - mx) for l in ls]
    inv = 1.0 / (ws[0] + ws[1] + ws[2])
    e = e_ref[...]

    def expand(wg):
        rem = wg * inv
        ax = None
        for _ in range(3):
            piece = rem.astype(BF16)
            rem = rem - piece.astype(F32)
            part = jnp.dot(piece, e, preferred_element_type=F32)
            ax = part if ax is None else ax + part
        return ax

    o_last = pairs[-1][0]
    a = o_last
    for (og, _), wg in zip(pairs[:-1], ws[:-1]):
        a = a + expand(wg) * (og - o_last)
    y = jnp.dot(a.astype(BF16), w_ref[...], preferred_element_type=F32)
    out_ref[...] = h_ref[...] + _rms(y, g_ref[...], RMS_EPS)


def _combine_out(os_, lses, w_bf16, g, h2d, b, s, tm=ROW_TILE):
    m, d = h2d.shape
    w = os_[0].shape[-1]
    tps = s // tm
    expand = np.zeros((LANES, w), np.float32)
    for hh in range(N_HEADS):
        expand[hh, hh * HEAD_DIM:(hh + 1) * HEAD_DIM] = 1.0
    grp = lambda a: pl.BlockSpec((None, a.shape[1], tm // a.shape[1], a.shape[3]), lambda bb, i: (bb, 0, i, 0))
    row = pl.BlockSpec((tm, d), lambda bb, i: (bb * tps + i, 0))
    full = lambda shp: pl.BlockSpec(shp, lambda bb, i: (0, 0))
    return pl.pallas_call(
        _combine_out_kernel,
        grid=(b, tps),
        in_specs=[grp(a) for a in os_] + [grp(a) for a in lses]
                 + [full((LANES, w)), full((w, d)), full((1, d)), row],
        out_specs=row,
        out_shape=jax.ShapeDtypeStruct((m, d), F32),
        scratch_shapes=[pltpu.VMEM((3, w // LANES, tm, LANES), F32), pltpu.VMEM((3, tm, LANES), F32)],
        compiler_params=_cparams(("arbitrary", "arbitrary")),
        name="combine_out_proj",
    )(*os_, *lses, jnp.asarray(expand, BF16), w_bf16, g.reshape(1, d), h2d)


def _ffn_kernel(*refs, tiles_per_seq, f_chunk, fuse_out_proj):
    if fuse_out_proj:
        a_ref, wo_ref, g_mix_ref, refs = refs[0], refs[1], refs[2], refs[3:]
    (h_ref, g_in_ref, wup_ref, cw_ref, cb_ref, wdn_ref, g_out_ref, out_ref,
     h_sc, xn_sc, halo_sc, acc_sc, act_sc) = refs[:13]
    u_bufs = refs[13:]
    tm = h_ref.shape[0]
    d_ff = wdn_ref.shape[0]
    nc = d_ff // f_chunk
    i = pl.program_id(0)
    if fuse_out_proj:
        mix = jnp.dot(a_ref[...], wo_ref[...], preferred_element_type=F32)
        h_sc[...] = h_ref[...] + _rms(mix, g_mix_ref[...], RMS_EPS)
    else:
        h_sc[...] = h_ref[...]
    xn_sc[...] = _rms(h_sc[...], g_in_ref[...], RMS_EPS).astype(BF16)
    seq_start = (i % tiles_per_seq) == 0
    inv_sqrt2 = 1.0 / math.sqrt(2.0)

    def cols(c, j):
        return slice(j * d_ff + c * f_chunk, j * d_ff + (c + 1) * f_chunk)

    def up(c, u_sc):
        for j in range(2):
            u_sc[j, SUBLANES:tm + SUBLANES, :] = jnp.dot(xn_sc[...], wup_ref[:, cols(c, j)],
                                                         preferred_element_type=F32)

    def conv(u_sc, c, j, lanes):
        cw = cw_ref[:, cols(c, j)][:, lanes]
        out = cb_ref[:, cols(c, j)][:, lanes]
        for t in range(CONV_WIDTH):
            off = SUBLANES - (CONV_WIDTH - 1 - t)
            out = out + cw[t:t + 1, :] * u_sc[j, off:tm + off, lanes]
        return out

    def down(c, u_sc):
        u_sc[:, 0:SUBLANES, :] = jnp.where(seq_start, 0.0, halo_sc[c])
        halo_sc[c] = u_sc[:, tm:tm + SUBLANES, :]
        g0 = c - c % FFN_DOWN_GROUP
        for lh in range(f_chunk // LANES):
            lanes = slice(lh * LANES, (lh + 1) * LANES)
            gate = conv(u_sc, c, 0, lanes)
            val = conv(u_sc, c, 1, lanes)
            act = (0.5 * gate * (1.0 + lax.erf(gate * inv_sqrt2)) * val).astype(BF16)
            a0 = (c - g0) * f_chunk + lh * LANES
            act_sc[:, a0:a0 + LANES] = act
        if c == nc - 1 or c % FFN_DOWN_GROUP == FFN_DOWN_GROUP - 1:
            k = (c + 1 - g0) * f_chunk
            part = jnp.dot(act_sc[:, 0:k], wdn_ref[g0 * f_chunk:(c + 1) * f_chunk, :], preferred_element_type=F32)
            if g0 == 0:
                acc_sc[...] = part
            else:
                acc_sc[...] += part

    for c in range(min(FFN_AHEAD, nc)):
        up(c, u_bufs[c])
    for c in range(nc):
        if c + FFN_AHEAD < nc:
            up(c + FFN_AHEAD, u_bufs[(c + FFN_AHEAD) % (FFN_AHEAD + 1)])
        down(c, u_bufs[c % (FFN_AHEAD + 1)])
    out_ref[...] = h_sc[...] + _rms(acc_sc[...], g_out_ref[...], RMS_EPS)


def _conv_ffn(h2d, g_in, w_up_bf16, conv_w, conv_b, w_down_bf16, g_out, seq, out_proj=None, tm=ROW_TILE, f_chunk=256):
    m, d = h2d.shape
    f2 = w_up_bf16.shape[1]
    d_ff = f2 // 2
    assert seq % tm == 0 and d_ff % f_chunk == 0 and CONV_WIDTH - 1 <= SUBLANES
    nc = d_ff // f_chunk
    full = lambda shp: pl.BlockSpec(shp, lambda i: (0, 0))
    row = lambda width: pl.BlockSpec((tm, width), lambda i: (i, 0))
    ins = [h2d, g_in.reshape(1, d), w_up_bf16, conv_w, conv_b.reshape(1, f2), w_down_bf16, g_out.reshape(1, d)]
    in_specs = [row(d), full((1, d)), full((d, f2)), full((CONV_WIDTH, f2)), full((1, f2)), full((d_ff, d)),
                full((1, d))]
    if out_proj is not None:
        a2d, w_out_bf16, g_mix = out_proj
        ins = [a2d, w_out_bf16, g_mix.reshape(1, d)] + ins
        in_specs = [row(a2d.shape[1]), full(w_out_bf16.shape), full((1, d))] + in_specs
    return pl.pallas_call(
        functools.partial(_ffn_kernel, tiles_per_seq=seq // tm, f_chunk=f_chunk, fuse_out_proj=out_proj is not None),
        grid=(m // tm,),
        in_specs=in_specs,
        out_specs=row(d),
        out_shape=jax.ShapeDtypeStruct((m, d), F32),
        scratch_shapes=[pltpu.VMEM((tm, d), F32), pltpu.VMEM((tm, d), BF16),
                        pltpu.VMEM((nc, 2, SUBLANES, f_chunk), F32), pltpu.VMEM((tm, d), F32),
                        pltpu.VMEM((tm, FFN_DOWN_GROUP * f_chunk), BF16)]
                       + [pltpu.VMEM((2, tm + SUBLANES, f_chunk), F32)] * (FFN_AHEAD + 1),
        compiler_params=_cparams(("arbitrary",)),
        name="conv_ffn",
    )(*ins)


def _diff_kernel(tbl_ref, q_ref, k_ref, v_ref, bias_ref, lq1, lk1, lq2, lk2, sg_ref, o_ref,
                 qs_sc, s_sc, m_sc, l_sc, acc_sc, *, tile, n_bias, far_bucket, lambda_init, heads):
    hg = pl.program_id(0)
    qi = pl.program_id(2)
    hd2 = 2 * HEAD_DIM
    lane = lax.broadcasted_iota(jnp.int32, (1, LANES), 1)
    for hh in range(heads):
        q = q_ref[:, hh * hd2:(hh + 1) * hd2]
        zero = jnp.zeros_like(q)
        qs_sc[hh, 0:tile, :] = jnp.where(lane < HEAD_DIM, q, zero)
        qs_sc[hh, tile:2 * tile, :] = jnp.where(lane >= HEAD_DIM, q, zero)
    m_sc[...] = jnp.full(m_sc.shape, NEG_INF, F32)
    l_sc[...] = jnp.zeros(l_sc.shape, F32)
    acc_sc[...] = jnp.zeros(acc_sc.shape, F32)
    ones = jnp.ones((tile, LANES), BF16)

    def scores(ki, hh, c):
        k = k_ref[pl.ds(pl.multiple_of(ki * tile, tile), tile), hh * hd2:(hh + 1) * hd2]
        return lax.dot_general(qs_sc[hh, c * tile:(c + 1) * tile, :], k, (((1,), (1,)), ((), ())),
                               preferred_element_type=F32)

    def update(hh, rows, s, vx):
        m_prev = m_sc[hh, rows, :]
        m_new = jnp.maximum(m_prev, jnp.max(s, axis=-1, keepdims=True))
        alpha = jnp.exp2(m_prev - m_new)
        p = jnp.exp2(s - jnp.concatenate([m_new] * (s.shape[1] // LANES), axis=1))
        pv = jnp.dot(p.astype(BF16), vx, preferred_element_type=F32)
        l_sc[hh, rows, :] = alpha * l_sc[hh, rows, :] + pv[:, hd2:]
        acc_sc[hh, rows, :] = alpha * acc_sc[hh, rows, :] + pv[:, :hd2]
        m_sc[hh, rows, :] = m_new

    def values(ki, hh):
        v = v_ref[pl.ds(pl.multiple_of(ki * tile, tile), tile), hh * hd2:(hh + 1) * hd2]
        return jnp.concatenate([v, ones], axis=1)

    def step(ki, far):
        for hh in range(heads):
            vx = values(ki, hh)
            if far:
                bias = tbl_ref[hg * heads + hh, far_bucket] * LOG2E
            else:
                bias = bias_ref[hh, jnp.minimum(qi - ki, n_bias - 1)]
            for c in range(2):
                rows = slice(c * tile, (c + 1) * tile)
                s = s_sc[hh, rows, :] + bias
                s_sc[hh, rows, :] = scores(ki + 1, hh, c)
                update(hh, rows, s, vx)

    def diagonal_step():
        half = tile // 2
        for hh in range(heads):
            vx = values(qi, hh)
            for c in range(2):
                for r0 in range(0, tile, half):
                    rows = slice(c * tile + r0, c * tile + r0 + half)
                    nk = r0 + half
                    s = s_sc[hh, rows, 0:nk] + bias_ref[hh, 0, r0:r0 + half, 0:nk]
                    update(hh, rows, s, vx[0:nk, :])

    for hh in range(heads):
        for c in range(2):
            s_sc[hh, c * tile:(c + 1) * tile, :] = scores(0, hh, c)

    def far_body(ki, carry):
        step(ki, True)
        return carry

    def near_body(ki, carry):
        step(ki, False)
        return carry

    if far_bucket is None:
        n_far = 0
    else:
        n_far = jnp.maximum(qi - (n_bias - 1), 0)
        lax.fori_loop(0, n_far, far_body, 0)
    lax.fori_loop(n_far, qi, near_body, 0)
    diagonal_step()

    lam = (jnp.exp(jnp.sum(lq1[...].astype(F32) * lk1[...].astype(F32), axis=-1, keepdims=True))
           - jnp.exp(jnp.sum(lq2[...].astype(F32) * lk2[...].astype(F32), axis=-1, keepdims=True))
           + lambda_init)
    for hh in range(heads):
        o = acc_sc[hh] * (1.0 / l_sc[hh])
        a = o[0:tile, :] - lam * o[tile:2 * tile, :]
        y = _rms(a, sg_ref[...], SUBLN_EPS) * (1.0 - lambda_init)
        o_ref[:, hh * hd2:(hh + 1) * hd2] = y.astype(o_ref.dtype)


def _diff_attention(q, k, v, table, bias, far_bucket, lq1, lk1, lq2, lk2, subln_g, lambda_init, tile, heads=DIFF_HEADS):
    b, s, _ = q.shape
    n_bias = bias.shape[1]
    hd2 = 2 * HEAD_DIM
    hw = heads * hd2
    ng = N_HEADS // heads
    vec = lambda a: a.reshape(1, -1)
    small = lambda n: pl.BlockSpec((1, n), lambda h, bb, i: (0, 0))
    return pl.pallas_call(
        functools.partial(_diff_kernel, tile=tile, n_bias=n_bias, far_bucket=far_bucket, lambda_init=lambda_init,
                          heads=heads),
        grid=(ng, b, s // tile),
        in_specs=[pl.BlockSpec(memory_space=pltpu.SMEM),
                  pl.BlockSpec((None, tile, hw), lambda h, bb, i: (bb, i, h)),
                  pl.BlockSpec((None, s, hw), lambda h, bb, i: (bb, 0, h)),
                  pl.BlockSpec((None, s, hw), lambda h, bb, i: (bb, 0, h)),
                  pl.BlockSpec((heads, n_bias, tile, tile), lambda h, bb, i: (h, 0, 0, 0),
                               pipeline_mode=pl.Buffered(1)),
                  small(HEAD_DIM), small(HEAD_DIM), small(HEAD_DIM), small(HEAD_DIM), small(hd2)],
        out_specs=pl.BlockSpec((None, tile, hw), lambda h, bb, i: (bb, i, h)),
        out_shape=jax.ShapeDtypeStruct((b, s, N_HEADS * hd2), BF16),
        scratch_shapes=[pltpu.VMEM((heads, 2 * tile, hd2), BF16), pltpu.VMEM((heads, 2 * tile, tile), F32),
                        pltpu.VMEM((heads, 2 * tile, LANES), F32), pltpu.VMEM((heads, 2 * tile, LANES), F32),
                        pltpu.VMEM((heads, 2 * tile, hd2), F32)],
        compiler_params=_cparams(("arbitrary", "arbitrary", "arbitrary")),
        name="diff_attn",
    )(table, q, k, v, bias, vec(lq1), vec(lk1), vec(lq2), vec(lk2), vec(subln_g))


DIFF_TILE = 512
DIL_ROWS = 2048
FFN_AHEAD = 8
FFN_DOWN_GROUP = 4


def kernel(x, rel_bias_table, norm_g, w_in_a, w_out_a, kv_norm_g, w_k_shared, w_v_shared, w_q_b,
           lam_q1, lam_k1, lam_q2, lam_k2, subln_g, w_out_b, w_up, conv_w, conv_b, w_down):
    b, s, d = x.shape
    depth = norm_g.shape[0]
    n_a = w_in_a.shape[0]
    bf = lambda a: a.astype(BF16)

    dil_bias = _bias_tiles(rel_bias_table, _dilated_bias_idx(), head_major=False, scale=LOG2E)
    diff_tile = min(DIFF_TILE, s)
    diff_idx, far_bucket = _diff_bias_idx(s, diff_tile)
    diff_bias = _bias_tiles(rel_bias_table, diff_idx, head_major=True, scale=LOG2E)

    h = x.reshape(b * s, d)
    k_sh = v_sh = q = None
    for layer in range(depth):
        g = norm_g[layer]
        if layer < n_a:
            qkvs = _proj_a(h, g[0], bf(w_in_a[layer]), b, s)
            outs = [_dilated_group(qkvs[gi], dil_bias, gi, dil) for gi, (_, dil) in enumerate(DIL_GROUPS)]
            h = _combine_out([o for o, _ in outs], [l for _, l in outs], bf(w_out_a[layer]), g[1], h, b, s)
            mixer_out = None
        else:
            j = layer - n_a
            lambda_init = 0.8 - 0.6 * math.exp(-0.3 * layer)
            if q is None:
                q, = _norm_matmul(h, [(g[0], bf(w_q_b[j]), Q_SCALE)])
            a = _diff_attention(q.reshape(b, s, -1), k_sh, v_sh, rel_bias_table, diff_bias, far_bucket, lam_q1[j],
                                lam_k1[j], lam_q2[j], lam_k2[j], subln_g[j], lambda_init, diff_tile)
            q = None
            mixer_out = (a.reshape(b * s, -1), bf(w_out_b[j]), g[1])
        h = _conv_ffn(h, g[2], bf(w_up[layer]), conv_w[layer], conv_b[layer], bf(w_down[layer]), g[3], s,
                      out_proj=mixer_out)
        if layer == n_a - 1:
            projs = [(kv_norm_g, bf(w_k_shared), 1.0), (kv_norm_g, bf(w_v_shared), 1.0)]
            if layer + 1 < depth:
                projs.append((norm_g[layer + 1, 0], bf(w_q_b[0]), Q_SCALE))
            res = _norm_matmul(h, projs)
            k_sh, v_sh = res[0].reshape(b, s, -1), res[1].reshape(b, s, -1)
            q = res[2] if len(res) > 2 else None
    return h.reshape(b, s, d)
```
